```python
import jax, jax.numpy as jnp
from jax import lax
import numpy as np

D_MODEL = 4096
BATCH = 2
SEQ = 8192
DEPTH = 1

CHUNK = 64
N_META = 16
N_PAD = CHUNK - N_META
EPS = 1e-6
NEG = -1e30

ML_HEADS = 8
ML_QK_DIM = D_MODEL // 16
ML_V_DIM = D_MODEL // 8
ML_QK_WIDTH = ML_HEADS * ML_QK_DIM
ML_WIDTH = ML_HEADS * ML_V_DIM
GATE_CAP = 15.0

GDN_HEADS = 32
GDN_HEAD_DIM = D_MODEL // 32
GDN_WIDTH = GDN_HEADS * GDN_HEAD_DIM
CONV_WIDTH = 4

IN_SPLITS = (ML_QK_WIDTH, ML_QK_WIDTH, ML_WIDTH, ML_WIDTH, ML_WIDTH, ML_HEADS, ML_HEADS,
             3 * GDN_WIDTH, GDN_WIDTH, GDN_HEADS, GDN_HEADS, D_MODEL, D_MODEL)
D_IN = sum(IN_SPLITS)

kernel_name = "hybrid_mlstm_gdn_gated_merge"


def rmsnorm(x, g):
    xf = x.astype(jnp.float32)
    y = xf * lax.rsqrt(jnp.mean(xf * xf, -1, keepdims=True) + EPS)
    return (y * g.astype(jnp.float32)).astype(x.dtype)


def l2norm(x):
    xf = x.astype(jnp.float32)
    return xf * lax.rsqrt(jnp.sum(xf * xf, -1, keepdims=True) + EPS)


def split_heads(t, n_heads):
    B, S = t.shape[:2]
    return t.reshape(B, S, n_heads, -1)


def chunk_heads(t):
    B, S, H = t.shape[:3]
    t = t.reshape(B, S // CHUNK, CHUNK, H, *t.shape[3:])
    perm = (1, 0, 3, 2) + tuple(range(4, t.ndim))
    return t.transpose(perm)


def unchunk_heads(t):
    N, B, H, L, d = t.shape
    return t.transpose(1, 0, 3, 2, 4).reshape(B, N * L, H, d)


def causal_depthwise_conv(u, w):
    K, C = w.shape
    return lax.conv_general_dilated(u, w[:, None, :].astype(u.dtype), window_strides=(1,),
                                    padding=[(K - 1, 0)], dimension_numbers=('NWC', 'WIO', 'NWC'),
                                    feature_group_count=C)


def mlstm_chunkwise(q, k, v, i_pre, logf):
    B, S, H, dk = q.shape
    dv = v.shape[-1]
    f32 = jnp.float32
    qc = chunk_heads(q.astype(f32))
    kc = chunk_heads(k.astype(f32)) * (dk ** -0.5)
    vc = chunk_heads(v.astype(f32))
    ic = chunk_heads(i_pre)
    bcum = jnp.cumsum(chunk_heads(logf), axis=-1)
    causal = jnp.tril(jnp.ones((CHUNK, CHUNK), bool))
    dmat = jnp.where(causal, bcum[..., :, None] - bcum[..., None, :] + ic[..., None, :], NEG)
    dmax = jnp.max(dmat, -1)
    s = jnp.einsum('nbhid,nbhjd->nbhij', qc, kc) * jnp.exp(dmat - dmax[..., None])
    num_intra = jnp.einsum('nbhij,nbhjd->nbhid', s, vc)
    den_intra = jnp.sum(s, -1)
    g_tot = bcum[..., -1]
    a_log = g_tot[..., None] - bcum + ic

    def step(carry, xs):
        C, n, m = carry
        q_, k_, v_, b_, a_, g_, dmax_, num_, den_ = xs
        inter = b_ + m[..., None]
        m_i = jnp.maximum(dmax_, inter)
        w_inter = jnp.exp(inter - m_i)
        w_intra = jnp.exp(dmax_ - m_i)
        num = w_inter[..., None] * jnp.einsum('bhid,bhde->bhie', q_, C) + w_intra[..., None] * num_
        den = w_inter * jnp.einsum('bhid,bhd->bhi', q_, n) + w_intra * den_
        h = num / jnp.maximum(jnp.abs(den), jnp.exp(-m_i))[..., None]
        m_new = jnp.maximum(g_ + m, jnp.max(a_, -1))
        decay = jnp.exp(g_ + m - m_new)
        wk = jnp.exp(a_ - m_new[..., None])[..., None] * k_
        C = decay[..., None, None] * C + jnp.einsum('bhld,bhle->bhde', wk, v_)
        n = decay[..., None] * n + jnp.sum(wk, -2)
        return (C, n, m_new), h

    init = (jnp.zeros((B, H, dk, dv), f32), jnp.zeros((B, H, dk), f32), jnp.zeros((B, H), f32))
    _, h = lax.scan(step, init, (qc, kc, vc, bcum, a_log, g_tot, dmax, num_intra, den_intra))
    return unchunk_heads(h)


def gated_delta_chunkwise(q, k, v, g, beta):
    B, S, H, dk = q.shape
    dv = v.shape[-1]
    f32 = jnp.float32
    qc = chunk_heads(q) * (dk ** -0.5)
    kc = chunk_heads(k)
    vc = chunk_heads(v.astype(f32))
    gc = jnp.cumsum(chunk_heads(g), -1)
    bc = chunk_heads(beta)
    incl = jnp.tril(jnp.ones((CHUNK, CHUNK), bool))
    strict = jnp.tril(jnp.ones((CHUNK, CHUNK), bool), -1)
    decay_incl = jnp.exp(jnp.where(incl, gc[..., :, None] - gc[..., None, :], NEG))
    kk = jnp.einsum('nbhid,nbhjd->nbhij', kc, kc)
    a_strict = jnp.where(strict, kk * decay_incl, 0.0) * bc[..., :, None]
    eye = jnp.eye(CHUNK, dtype=f32)
    rhs = jnp.concatenate([vc * bc[..., None], kc * (bc * jnp.exp(gc))[..., None]], -1)
    uw = lax.linalg.triangular_solve(eye + a_strict, rhs, left_side=True, lower=True,
                                     unit_diagonal=True)
    u, w = uw[..., :dv], uw[..., dv:]
    qk = jnp.einsum('nbhid,nbhjd->nbhij', qc, kc) * decay_incl
    q_dec = qc * jnp.exp(gc)[..., None]
    g_last = gc[..., -1]
    k_dec = kc * jnp.exp(g_last[..., None] - gc)[..., None]

    def step(S_, xs):
        qk_, qd_, kd_, u_, w_, gl_ = xs
        v_new = u_ - jnp.einsum('bhld,bhde->bhle', w_, S_)
        o = jnp.einsum('bhld,bhde->bhle', qd_, S_) + jnp.einsum('bhij,bhje->bhie', qk_, v_new)
        S_ = jnp.exp(gl_)[..., None, None] * S_ + jnp.einsum('bhld,bhle->bhde', kd_, v_new)
        return S_, o

    _, o = lax.scan(step, jnp.zeros((B, H, dk, dv), f32), (qk, q_dec, k_dec, u, w, g_last))
    return unchunk_heads(o)


def hybrid_layer(x, valid, norm_g, w_in, b_igate, b_fgate, ml_norm_g, conv_w, a_log, dt_bias,
                 gdn_norm_g, w_proj_a, w_proj_b, w_out):
    B, S, _ = x.shape
    f32 = jnp.float32
    vmask = valid[None, :, None]
    h = rmsnorm(x, norm_g)
    proj = h @ w_in.astype(h.dtype)
    (m_q, m_k, m_v, m_o, m_z, m_i, m_f, g_qkv, g_z, g_a, g_b, gate_a, gate_b) = jnp.split(
        proj, np.cumsum(IN_SPLITS)[:-1].tolist(), axis=-1)

    i_pre = GATE_CAP * jnp.tanh((m_i.astype(f32) + b_igate) / GATE_CAP)
    f_pre = GATE_CAP * jnp.tanh((m_f.astype(f32) + b_fgate) / GATE_CAP)
    i_pre = jnp.where(vmask, i_pre, NEG)
    logf = jnp.where(vmask, jax.nn.log_sigmoid(f_pre), 0.0)
    hm = mlstm_chunkwise(split_heads(m_q, ML_HEADS), split_heads(m_k, ML_HEADS),
                         split_heads(m_v, ML_HEADS), i_pre, logf)
    hm = rmsnorm(hm, ml_norm_g.reshape(ML_HEADS, ML_V_DIM)).reshape(B, S, ML_WIDTH)
    y_a = (hm * jax.nn.sigmoid(m_o.astype(f32)) * jax.nn.silu(m_z.astype(f32))).astype(x.dtype)

    qkv = jax.nn.silu(causal_depthwise_conv(g_qkv, conv_w))
    c_q, c_k, c_v = jnp.split(qkv, 3, axis=-1)
    c_q = l2norm(split_heads(c_q, GDN_HEADS))
    c_k = l2norm(split_heads(c_k, GDN_HEADS))
    c_v = split_heads(c_v, GDN_HEADS)
    beta = jnp.where(vmask, jax.nn.sigmoid(g_b.astype(f32)), 0.0)
    g = jnp.where(vmask, -jnp.exp(a_log.astype(f32)) * jax.nn.softplus(g_a.astype(f32) + dt_bias), 0.0)
    og = gated_delta_chunkwise(c_q, c_k, c_v, g, beta)
    og = rmsnorm(og, gdn_norm_g) * jax.nn.silu(split_heads(g_z, GDN_HEADS).astype(f32))
    y_b = og.reshape(B, S, GDN_WIDTH).astype(x.dtype)

    merged = (jax.nn.sigmoid(gate_a) * (y_a @ w_proj_a.astype(x.dtype))
              + jax.nn.sigmoid(gate_b) * (y_b @ w_proj_b.astype(x.dtype)))
    return x + merged @ w_out.astype(x.dtype)


def setup_inputs(seed: int = 0) -> dict:
    key = jax.random.key(seed)
    ks = jax.random.split(key, 16)
    f32 = jnp.float32
    nrm = lambda k, s: jax.random.normal(k, s, f32)
    x = nrm(ks[0], (BATCH, SEQ, D_MODEL))
    meta = nrm(ks[1], (N_META, D_MODEL))
    norm_in_g = 1.0 + 0.02 * nrm(ks[2], (DEPTH, D_MODEL))
    w_in = nrm(ks[3], (DEPTH, D_MODEL, D_IN)) * (D_MODEL ** -0.5)
    b_igate = 0.1 * nrm(ks[4], (DEPTH, ML_HEADS))
    b_fgate = 3.0 + 0.5 * nrm(ks[5], (DEPTH, ML_HEADS))
    ml_norm_g = 1.0 + 0.02 * nrm(ks[6], (DEPTH, ML_WIDTH))
    conv_w = nrm(ks[7], (DEPTH, CONV_WIDTH, 3 * GDN_WIDTH)) * (CONV_WIDTH ** -0.5)
    a_log = jnp.log(jax.random.uniform(ks[8], (DEPTH, GDN_HEADS), f32, 1.0, 16.0))
    dt = jnp.exp(jax.random.uniform(ks[9], (DEPTH, GDN_HEADS), f32, np.log(1e-3), np.log(1e-1)))
    dt_bias = dt + jnp.log(-jnp.expm1(-dt))
    gdn_norm_g = 1.0 + 0.02 * nrm(ks[10], (DEPTH, GDN_HEAD_DIM))
    w_proj_a = nrm(ks[11], (DEPTH, ML_WIDTH, D_MODEL)) * (ML_WIDTH ** -0.5)
    w_proj_b = nrm(ks[12], (DEPTH, GDN_WIDTH, D_MODEL)) * (GDN_WIDTH ** -0.5)
    w_out = nrm(ks[13], (DEPTH, D_MODEL, D_MODEL)) * (D_MODEL ** -0.5)
    norm_f_g = 1.0 + 0.02 * nrm(ks[14], (D_MODEL,))
    return {"x": x, "meta": meta, "norm_in_g": norm_in_g, "w_in": w_in, "b_igate": b_igate,
            "b_fgate": b_fgate, "ml_norm_g": ml_norm_g, "conv_w": conv_w, "a_log": a_log,
            "dt_bias": dt_bias, "gdn_norm_g": gdn_norm_g, "w_proj_a": w_proj_a,
            "w_proj_b": w_proj_b, "w_out": w_out, "norm_f_g": norm_f_g}


def reference(x, meta, norm_in_g, w_in, b_igate, b_fgate, ml_norm_g, conv_w, a_log, dt_bias,
              gdn_norm_g, w_proj_a, w_proj_b, w_out, norm_f_g):
    B = x.shape[0]
    dtype = x.dtype
    h = jnp.concatenate([jnp.zeros((B, N_PAD, D_MODEL), dtype),
                         jnp.broadcast_to(meta.astype(dtype), (B, N_META, D_MODEL)), x], axis=1)
    valid = jnp.arange(h.shape[1]) >= N_PAD
    for l in range(DEPTH):
        h = hybrid_layer(h, valid, norm_in_g[l], w_in[l], b_igate[l], b_fgate[l], ml_norm_g[l],
                         conv_w[l], a_log[l], dt_bias[l], gdn_norm_g[l], w_proj_a[l], w_proj_b[l],
                         w_out[l])
    h = rmsnorm(h, norm_f_g)
    return h[:, N_PAD + N_META:]
```

```python
import functools

import jax
import jax.numpy as jnp
from jax import lax
from jax.experimental import pallas as pl
from jax.experimental.pallas import tpu as pltpu

CHUNK = 64
N_META = 16
N_PAD = CHUNK - N_META
EPS = 1e-6
NEG = -1e30
GATE_CAP = 15.0
LANES = 128
VMEM_LIMIT = 56 * 1024 * 1024

F32 = jnp.float32
BF16 = jnp.bfloat16

(_P_C, _P_WINTER, _P_FLOOR, _P_WKF, _P_DECAY, _P_GC, _P_BETA, _P_EXPGC, _P_KDECF, _P_EXPGL,
 _P_E) = range(11)
_N_PLANES = 11


def _sigmoid(x):
    return 1.0 / (1.0 + jnp.exp(-x))


def _softplus(x):
    return jnp.maximum(x, 0.0) + jnp.log1p(jnp.exp(-jnp.abs(x)))


def _pick(n, cands):
    for c in cands:
        if c <= n and n % c == 0:
            return c
    return n


def _params(sem, vmem=None):
    return pltpu.CompilerParams(dimension_semantics=sem, vmem_limit_bytes=vmem)


def _rmsnorm_kernel(x_ref, g_ref, o_ref):
    x = x_ref[...].astype(F32)
    ms = jnp.mean(x * x, axis=-1, keepdims=True)
    o_ref[...] = (x * lax.rsqrt(ms + EPS) * g_ref[...]).astype(o_ref.dtype)


def _rmsnorm(x2d, g, out_dtype):
    m, d = x2d.shape
    tr = _pick(m, (256, 128, 64))
    return pl.pallas_call(
        _rmsnorm_kernel,
        grid=(m // tr,),
        in_specs=[pl.BlockSpec((tr, d), lambda i: (i, 0)),
                  pl.BlockSpec((1, d), lambda i: (0, 0))],
        out_specs=pl.BlockSpec((tr, d), lambda i: (i, 0)),
        out_shape=jax.ShapeDtypeStruct((m, d), out_dtype),
        compiler_params=_params(("parallel",)),
        name="rmsnorm",
    )(x2d, g.reshape(1, d).astype(F32))


def _inproj_kernel(h_ref, w_ref, ws_ref, p_ref, gs_ref):
    h = h_ref[...]
    p_ref[...] = jnp.dot(h, w_ref[...], preferred_element_type=F32).astype(p_ref.dtype)

    @pl.when(pl.program_id(1) == 0)
    def _():
        gs_ref[...] = jnp.dot(h, ws_ref[...], preferred_element_type=F32)


def _inproj(hn, w_big, w_small):
    m, d = hn.shape
    n = w_big.shape[1]
    tm = _pick(m, (1024, 512, 256, 128, 64))
    tn = _pick(n, (1024, 512, 256, 128))
    return pl.pallas_call(
        _inproj_kernel,
        grid=(m // tm, n // tn),
        in_specs=[pl.BlockSpec((tm, d), lambda i, j: (i, 0)),
                  pl.BlockSpec((d, tn), lambda i, j: (0, j)),
                  pl.BlockSpec((d, LANES), lambda i, j: (0, 0))],
        out_specs=[pl.BlockSpec((tm, tn), lambda i, j: (i, j)),
                   pl.BlockSpec((tm, LANES), lambda i, j: (i, 0))],
        out_shape=[jax.ShapeDtypeStruct((m, n), BF16),
                   jax.ShapeDtypeStruct((m, LANES), F32)],
        compiler_params=_params(("parallel", "arbitrary"), VMEM_LIMIT),
        name="inproj",
    )(hn, w_big, w_small)


def _cumsum_rows(x, rowi):
    for d in (1, 2, 4, 8, 16, 32):
        x = x + jnp.where(rowi >= d, pltpu.roll(x, d, axis=0), 0.0)
    return x


def _cummax_rows(x, rowi):
    for d in (1, 2, 4, 8, 16, 32):
        x = jnp.maximum(x, jnp.where(rowi >= d, pltpu.roll(x, d, axis=0), NEG))
    return x


def _gates_kernel(gin_ref, prm_ref, out_ref, m_ref, *, tc):
    t = pl.program_id(1)

    @pl.when(t == 0)
    def _():
        m_ref[...] = jnp.zeros_like(m_ref)

    rowi = lax.broadcasted_iota(jnp.int32, (CHUNK, LANES), 0)
    b_i = prm_ref[0:1, :]
    b_f = prm_ref[1:2, :]
    neg_a = -jnp.exp(prm_ref[2:3, :])
    dtb = prm_ref[3:4, :]
    last = CHUNK - 1
    for ci in range(tc):
        rows = slice(ci * CHUNK, (ci + 1) * CHUNK)
        valid = ((t * tc + ci) * CHUNK + rowi) >= N_PAD
        m_i = gin_ref[0, 0, rows, :]
        m_f = gin_ref[0, 1, rows, :]
        g_a = gin_ref[0, 2, rows, :]
        g_b = gin_ref[0, 3, rows, :]

        i_pre = GATE_CAP * jnp.tanh((m_i + b_i) / GATE_CAP)
        f_pre = GATE_CAP * jnp.tanh((m_f + b_f) / GATE_CAP)
        i_pre = jnp.where(valid, i_pre, NEG)
        logf = jnp.where(valid, -_softplus(-f_pre), 0.0)
        bcum = _cumsum_rows(logf, rowi)
        e = i_pre - bcum
        cm = _cummax_rows(e, rowi)
        gtot = bcum[last:last + 1, :]
        m_old = m_ref[0:1, :]
        m_new = jnp.maximum(gtot + m_old, gtot + cm[last:last + 1, :])
        c = jnp.maximum(cm, m_old)
        out_ref[0, _P_C, rows, :] = c
        out_ref[0, _P_WINTER, rows, :] = jnp.exp(m_old - c)
        out_ref[0, _P_FLOOR, rows, :] = jnp.exp(-(bcum + c))
        out_ref[0, _P_WKF, rows, :] = jnp.exp(gtot + e - m_new)
        out_ref[0, _P_DECAY, rows, :] = jnp.broadcast_to(jnp.exp(gtot + m_old - m_new), (CHUNK, LANES))
        out_ref[0, _P_E, rows, :] = e
        m_ref[...] = jnp.broadcast_to(m_new, m_ref.shape)

        g = jnp.where(valid, neg_a * _softplus(g_a + dtb), 0.0)
        beta = jnp.where(valid, _sigmoid(g_b), 0.0)
        gc = _cumsum_rows(g, rowi)
        gl = gc[last:last + 1, :]
        out_ref[0, _P_GC, rows, :] = gc
        out_ref[0, _P_BETA, rows, :] = beta
        out_ref[0, _P_EXPGC, rows, :] = jnp.exp(gc)
        out_ref[0, _P_KDECF, rows, :] = jnp.exp(gl - gc)
        out_ref[0, _P_EXPGL, rows, :] = jnp.broadcast_to(jnp.exp(gl), (CHUNK, LANES))


def _gates(gin, prm):
    b, _, s_full, _ = gin.shape
    n_full = s_full // CHUNK
    tc = _pick(n_full, (3, 4, 2, 1))
    return pl.pallas_call(
        functools.partial(_gates_kernel, tc=tc),
        grid=(b, n_full // tc),
        in_specs=[pl.BlockSpec((1, 4, tc * CHUNK, LANES), lambda i, t: (i, 0, t, 0)),
                  pl.BlockSpec((8, LANES), lambda i, t: (0, 0))],
        out_specs=pl.BlockSpec((1, _N_PLANES, tc * CHUNK, LANES), lambda i, t: (i, 0, t, 0)),
        out_shape=jax.ShapeDtypeStruct((b, _N_PLANES, s_full, LANES), F32),
        scratch_shapes=[pltpu.VMEM((8, LANES), F32)],
        compiler_params=_params(("arbitrary", "arbitrary")),
        name="gates",
    )(gin, prm)


def _conv_kernel(x_ref, halo_ref, first_ref, w_ref, o_ref, *, r, cb, dg, ncb, scale):
    i = pl.program_id(1)
    cbi = pl.program_id(2)
    w = w_ref[...]
    is_v = cbi >= 2 * ncb
    qscale = jnp.where(cbi < ncb, scale, 1.0).astype(F32)
    halo = jnp.where(i == 0, first_ref[...], halo_ref[...]).astype(F32)
    hb = 16

    def body(s, carry):
        r0 = pl.multiple_of(s * CHUNK, CHUNK)
        cur = x_ref[pl.ds(r0, CHUNK), :].astype(F32)
        pstart = pl.multiple_of(jnp.maximum(r0 - hb, 0), hb)
        prev = jnp.where(s == 0, halo, x_ref[pl.ds(pstart, hb), :].astype(F32))
        ext = jnp.concatenate([prev, cur], axis=0)
        y = (w[3:4, :] * ext[hb:hb + CHUNK] + w[2:3, :] * ext[hb - 1:hb - 1 + CHUNK]
             + w[1:2, :] * ext[hb - 2:hb - 2 + CHUNK] + w[0:1, :] * ext[hb - 3:hb - 3 + CHUNK])
        a = y * _sigmoid(y)
        for hd in range(cb // dg):
            seg = a[:, hd * dg:(hd + 1) * dg]
            ss = jnp.sum(seg * seg, axis=1, keepdims=True)
            fac = jnp.where(is_v, 1.0, lax.rsqrt(ss + EPS) * qscale)
            o_ref[pl.ds(r0, CHUNK), hd * dg:(hd + 1) * dg] = (seg * fac).astype(o_ref.dtype)
        return carry

    lax.fori_loop(0, r // CHUNK, body, 0)


def _conv(p, first, conv_w, *, b, s, d, dg, col0):
    r = _pick(s, (512, 256, 128, 64))
    cb = _pick(d, (1024, 512, 256, 128))
    cb = max(cb, dg)
    ncb = d // cb
    nr = s // r
    cbase = col0 // cb
    hb = 16
    return pl.pallas_call(
        functools.partial(_conv_kernel, r=r, cb=cb, dg=dg, ncb=ncb, scale=float(dg) ** -0.5),
        grid=(b, nr, 3 * ncb),
        in_specs=[pl.BlockSpec((r, cb), lambda bi, i, c: (bi * nr + i, cbase + c)),
                  pl.BlockSpec((hb, cb), lambda bi, i, c: (jnp.maximum((bi * s + i * r) // hb - 1, 0), cbase + c)),
                  pl.BlockSpec((hb, cb), lambda bi, i, c: (0, c)),
                  pl.BlockSpec((4, cb), lambda bi, i, c: (0, c))],
        out_specs=pl.BlockSpec((r, cb), lambda bi, i, c: (bi * nr + i, c)),
        out_shape=jax.ShapeDtypeStruct((b * s, 3 * d), BF16),
        compiler_params=_params(("parallel", "parallel", "parallel")),
        name="conv",
    )(p, p, first, conv_w.astype(F32))


def _mlstm_kernel(q_ref, k_ref, v_ref, o_ref, z_ref, colm_ref, erow_ref, c0_ref, n0_ref, g_ref,
                  ya_ref, c_ref, n_ref, *, t_chunks, hm, dqk, dv, scale):
    @pl.when(pl.program_id(1) == 0)
    def _():
        c_ref[...] = c0_ref[...]
        n_ref[...] = n0_ref[...]

    ri = lax.broadcasted_iota(jnp.int32, (CHUNK, CHUNK), 0)
    ci = lax.broadcasted_iota(jnp.int32, (CHUNK, CHUNK), 1)
    causal = ci <= ri

    def chunk_body(c, carry):
        r0 = pl.multiple_of(c * CHUNK, CHUNK)
        rows = pl.ds(r0, CHUNK)
        colm = colm_ref[0, rows, :]
        for h in range(hm):
            qs = slice(h * dqk, (h + 1) * dqk)
            vs = slice(h * dv, (h + 1) * dv)
            q = q_ref[rows, qs]
            k = k_ref[rows, qs]
            v = v_ref[rows, vs]
            cc = colm[:, _P_C * hm + h:_P_C * hm + h + 1]
            winter = colm[:, _P_WINTER * hm + h:_P_WINTER * hm + h + 1]
            floor = colm[:, _P_FLOOR * hm + h:_P_FLOOR * hm + h + 1]
            wkf = colm[:, _P_WKF * hm + h:_P_WKF * hm + h + 1]
            decay = colm[0:1, _P_DECAY * hm + h:_P_DECAY * hm + h + 1]
            e_row = erow_ref[0, c, h:h + 1, :]

            qk = lax.dot_general(q, k, (((1,), (1,)), ((), ())), preferred_element_type=F32) * scale
            s = qk * jnp.exp(jnp.where(causal, e_row - cc, NEG))
            den_intra = jnp.sum(s, axis=1, keepdims=True)
            c_st = c_ref[h]
            n_st = n_ref[h]
            num = winter * jnp.dot(q, c_st.astype(BF16), preferred_element_type=F32) + jnp.dot(
                s.astype(BF16), v, preferred_element_type=F32)
            qn = jnp.sum(q.astype(F32) * n_st, axis=1, keepdims=True)
            den = winter * qn + den_intra
            hh = num * (1.0 / jnp.maximum(jnp.abs(den), floor))
            ms = jnp.mean(hh * hh, axis=1, keepdims=True)
            hn = hh * lax.rsqrt(ms + EPS) * g_ref[:, vs]
            og = o_ref[rows, vs].astype(F32)
            zg = z_ref[rows, vs].astype(F32)
            ya_ref[rows, vs] = (hn * _sigmoid(og) * (zg * _sigmoid(zg))).astype(ya_ref.dtype)

            wk = k.astype(F32) * (wkf * scale)
            upd = lax.dot_general(wk.astype(BF16), v, (((0,), (0,)), ((), ())), preferred_element_type=F32)
            c_ref[h] = decay * c_st + upd
            n_ref[h] = decay * n_st + jnp.sum(wk, axis=0, keepdims=True)
        return carry

    lax.fori_loop(0, t_chunks, chunk_body, 0)


def _mlstm(p, colm, erow, c0, n0, g, *, b, s, d, hm):
    dqk = d // 2 // hm
    dv = d // hm
    t_chunks = _pick(s // CHUNK, (4, 2, 1))
    rb = t_chunks * CHUNK
    nt = s // rb
    hw = d // 2
    row = lambda bi, t: bi * nt + t
    return pl.pallas_call(
        functools.partial(_mlstm_kernel, t_chunks=t_chunks, hm=hm, dqk=dqk, dv=dv, scale=float(dqk) ** -0.5),
        grid=(b, nt),
        in_specs=[pl.BlockSpec((rb, hw), lambda bi, t: (row(bi, t), 0)),
                  pl.BlockSpec((rb, hw), lambda bi, t: (row(bi, t), 1)),
                  pl.BlockSpec((rb, d), lambda bi, t: (row(bi, t), 1)),
                  pl.BlockSpec((rb, d), lambda bi, t: (row(bi, t), 2)),
                  pl.BlockSpec((rb, d), lambda bi, t: (row(bi, t), 3)),
                  pl.BlockSpec((1, rb, LANES), lambda bi, t: (bi, t, 0)),
                  pl.BlockSpec((1, t_chunks, hm, CHUNK), lambda bi, t: (bi, t, 0, 0)),
                  pl.BlockSpec((hm, dqk, dv), lambda bi, t: (0, 0, 0)),
                  pl.BlockSpec((hm, 1, dqk), lambda bi, t: (0, 0, 0)),
                  pl.BlockSpec((1, d), lambda bi, t: (0, 0))],
        out_specs=[pl.BlockSpec((rb, d), lambda bi, t: (row(bi, t), 0)),
                   pl.BlockSpec((None, hm, dqk, dv), lambda bi, t: (bi, 0, 0, 0)),
                   pl.BlockSpec((None, hm, 1, dqk), lambda bi, t: (bi, 0, 0, 0))],
        out_shape=[jax.ShapeDtypeStruct((b * s, d), BF16),
                   jax.ShapeDtypeStruct((b, hm, dqk, dv), F32),
                   jax.ShapeDtypeStruct((b, hm, 1, dqk), F32)],
        compiler_params=_params(("arbitrary", "arbitrary"), VMEM_LIMIT),
        name="mlstm",
    )(p, p, p, p, p, colm, erow, c0, n0, g.reshape(1, d).astype(F32))


def _gdn_kernel(q_ref, k_ref, v_ref, z_ref, colg_ref, grow_ref, s0_ref, g_ref, yb_ref, s_ref,
                *, t_chunks, hg, dg):
    @pl.when(pl.program_id(2) == 0)
    def _():
        s_ref[...] = s0_ref[...]

    ri = lax.broadcasted_iota(jnp.int32, (CHUNK, CHUNK), 0)
    ci = lax.broadcasted_iota(jnp.int32, (CHUNK, CHUNK), 1)
    incl = ci <= ri
    strict = ci < ri
    eye = (ci == ri).astype(F32)
    tdims = (((0,), (0,)), ((), ()))
    ndims = (((1,), (1,)), ((), ()))

    def chunk_body(c, carry):
        r0 = pl.multiple_of(c * CHUNK, CHUNK)
        rows = pl.ds(r0, CHUNK)
        colg = colg_ref[0, rows, :]
        for h in range(hg):
            sl = slice(h * dg, (h + 1) * dg)
            q = q_ref[rows, sl]
            k = k_ref[rows, sl]
            v = v_ref[rows, sl]
            gc = colg[:, h:h + 1]
            beta = colg[:, hg + h:hg + h + 1]
            expgc = colg[:, 2 * hg + h:2 * hg + h + 1]
            kdecf = colg[:, 3 * hg + h:3 * hg + h + 1]
            expgl = colg[0:1, 4 * hg + h:4 * hg + h + 1]
            gc_row = grow_ref[0, c, h:h + 1, :]

            dec = jnp.exp(jnp.where(incl, gc - gc_row, NEG))
            kk = lax.dot_general(k, k, ndims, preferred_element_type=F32)
            qk = lax.dot_general(q, k, ndims, preferred_element_type=F32) * dec
            a = jnp.where(strict, kk * dec, 0.0) * beta
            pw = a
            tm = eye - a
            for _ in range(5):
                pb = pw.astype(BF16)
                pw = jnp.dot(pb, pb, preferred_element_type=F32)
                tm = tm + jnp.dot(tm.astype(BF16), pw.astype(BF16), preferred_element_type=F32)
            kf = k.astype(F32)
            rhs = jnp.concatenate([v.astype(F32) * beta, kf * (beta * expgc)], axis=1).astype(BF16)
            uw = jnp.dot(tm.astype(BF16), rhs, preferred_element_type=F32)
            u = uw[:, :dg]
            w = uw[:, dg:]
            s_st = s_ref[h]
            s_b = s_st.astype(BF16)
            v_new = u - jnp.dot(w.astype(BF16), s_b, preferred_element_type=F32)
            v_nb = v_new.astype(BF16)
            q_dec = (q.astype(F32) * expgc).astype(BF16)
            o = jnp.dot(q_dec, s_b, preferred_element_type=F32) + jnp.dot(
                qk.astype(BF16), v_nb, preferred_element_type=F32)
            k_dec = (kf * kdecf).astype(BF16)
            s_ref[h] = expgl * s_st + lax.dot_general(k_dec, v_nb, tdims, preferred_element_type=F32)

            ms = jnp.mean(o * o, axis=1, keepdims=True)
            zg = z_ref[rows, sl].astype(F32)
            yb_ref[rows, sl] = (o * lax.rsqrt(ms + EPS) * g_ref[...] * (zg * _sigmoid(zg))).astype(yb_ref.dtype)
        return carry

    lax.fori_loop(0, t_chunks, chunk_body, 0)


def _gdn(qkv, p, colg, grow, s0, g, *, b, s, d, hgd, hg, zcol0):
    dg = d // hgd
    groups = hgd // hg
    gw = hg * dg
    t_chunks = _pick(s // CHUNK, (4, 2, 1))
    rb = t_chunks * CHUNK
    nt = s // rb
    zb = zcol0 // gw
    row = lambda bi, t: bi * nt + t
    return pl.pallas_call(
        functools.partial(_gdn_kernel, t_chunks=t_chunks, hg=hg, dg=dg),
        grid=(b, groups, nt),
        in_specs=[pl.BlockSpec((rb, gw), lambda bi, gi, t: (row(bi, t), gi)),
                  pl.BlockSpec((rb, gw), lambda bi, gi, t: (row(bi, t), groups + gi)),
                  pl.BlockSpec((rb, gw), lambda bi, gi, t: (row(bi, t), 2 * groups + gi)),
                  pl.BlockSpec((rb, gw), lambda bi, gi, t: (row(bi, t), zb + gi)),
                  pl.BlockSpec((1, rb, LANES), lambda bi, gi, t: (bi, t, gi)),
                  pl.BlockSpec((1, t_chunks, hg, CHUNK), lambda bi, gi, t: (bi, t, gi, 0)),
                  pl.BlockSpec((hg, dg, dg), lambda bi, gi, t: (gi, 0, 0)),
                  pl.BlockSpec((1, dg), lambda bi, gi, t: (0, 0))],
        out_specs=[pl.BlockSpec((rb, gw), lambda bi, gi, t: (row(bi, t), gi)),
                   pl.BlockSpec((None, hg, dg, dg), lambda bi, gi, t: (bi, gi, 0, 0))],
        out_shape=[jax.ShapeDtypeStruct((b * s, d), BF16),
                   jax.ShapeDtypeStruct((b, hgd, dg, dg), F32)],
        compiler_params=_params(("arbitrary", "arbitrary", "arbitrary"), VMEM_LIMIT),
        name="gdn",
    )(qkv, qkv, qkv, p, colg, grow, s0, g.reshape(1, dg).astype(F32))


def _merge_kernel(ya_ref, yb_ref, wa_ref, wb_ref, ga_ref, gb_ref, o_ref):
    pa = jnp.dot(ya_ref[...], wa_ref[...], preferred_element_type=F32)
    pb = jnp.dot(yb_ref[...], wb_ref[...], preferred_element_type=F32)
    o_ref[...] = (_sigmoid(ga_ref[...].astype(F32)) * pa + _sigmoid(gb_ref[...].astype(F32)) * pb).astype(o_ref.dtype)


def _merge(ya, yb, wa, wb, p, *, gcol0):
    m, d = ya.shape
    tm = _pick(m, (256, 128, 64))
    tn = _pick(d, (1024, 512, 256, 128))
    ga0 = gcol0 // tn
    gb0 = (gcol0 + d) // tn
    return pl.pallas_call(
        _merge_kernel,
        grid=(d // tn, m // tm),
        in_specs=[pl.BlockSpec((tm, d), lambda j, i: (i, 0)),
                  pl.BlockSpec((tm, d), lambda j, i: (i, 0)),
                  pl.BlockSpec((d, tn), lambda j, i: (0, j)),
                  pl.BlockSpec((d, tn), lambda j, i: (0, j)),
                  pl.BlockSpec((tm, tn), lambda j, i: (i, ga0 + j)),
                  pl.BlockSpec((tm, tn), lambda j, i: (i, gb0 + j))],
        out_specs=pl.BlockSpec((tm, tn), lambda j, i: (i, j)),
        out_shape=jax.ShapeDtypeStruct((m, d), BF16),
        compiler_params=_params(("parallel", "parallel"), VMEM_LIMIT),
        name="merge",
    )(ya, yb, wa, wb, p, p)


def _outproj_kernel(m_ref, w_ref, x_ref, o_ref):
    o_ref[...] = x_ref[...] + jnp.dot(m_ref[...], w_ref[...], preferred_element_type=F32)


def _outproj(merged, wo, x2d):
    m, d = merged.shape
    tm = _pick(m, (1024, 512, 256, 128, 64))
    tn = _pick(d, (512, 256, 128))
    return pl.pallas_call(
        _outproj_kernel,
        grid=(m // tm, d // tn),
        in_specs=[pl.BlockSpec((tm, d), lambda i, j: (i, 0)),
                  pl.BlockSpec((d, tn), lambda i, j: (0, j)),
                  pl.BlockSpec((tm, tn), lambda i, j: (i, j))],
        out_specs=pl.BlockSpec((tm, tn), lambda i, j: (i, j)),
        out_shape=jax.ShapeDtypeStruct((m, d), F32),
        compiler_params=_params(("parallel", "parallel"), VMEM_LIMIT),
        name="outproj",
    )(merged, wo, x2d)


def _pad_lanes(a, width=LANES):
    return jnp.pad(a, [(0, 0)] * (a.ndim - 1) + [(0, width - a.shape[-1])])


def kernel(x, meta, norm_in_g, w_in, b_igate, b_fgate, ml_norm_g, conv_w, a_log, dt_bias,
           gdn_norm_g, w_proj_a, w_proj_b, w_out, norm_f_g):
    assert norm_in_g.shape[0] == 1, "single-layer block"
    b, s, d = x.shape
    hm = b_igate.shape[-1]
    hgd = a_log.shape[-1]
    dg = d // hgd
    hg = min(8, hgd)
    groups = hgd // hg
    assert s % CHUNK == 0 and meta.shape[0] == N_META and 2 * hm + 2 * hgd <= LANES
    n_chunks = s // CHUNK
    s_full = s + CHUNK

    w = w_in[0]
    g0 = 4 * d
    g1 = g0 + 2 * hm
    g2 = g1 + 4 * d
    g3 = g2 + 2 * hgd
    w_big = jnp.concatenate([w[:, :g0], w[:, g1:g2], w[:, g3:]], axis=1).astype(BF16)
    w_small = _pad_lanes(jnp.concatenate([w[:, g0:g1], w[:, g2:g3]], axis=1)).astype(BF16)
    wa = w_proj_a[0].astype(BF16)
    wb = w_proj_b[0].astype(BF16)
    wo = w_out[0].astype(BF16)

    x2d = x.reshape(b * s, d)
    h0 = jnp.concatenate([jnp.zeros((N_PAD, d), x.dtype), meta.astype(x.dtype)], axis=0)
    p_main, gs_main = _inproj(_rmsnorm(x2d, norm_in_g[0], BF16), w_big, w_small)
    p_meta, gs_meta = _inproj(_rmsnorm(h0, norm_in_g[0], BF16), w_big, w_small)

    gs_full = jnp.concatenate([jnp.broadcast_to(gs_meta[None], (b, CHUNK, LANES)),
                               gs_main.reshape(b, s, LANES)], axis=1)
    gin = jnp.stack([_pad_lanes(gs_full[..., 0:hm]), _pad_lanes(gs_full[..., hm:2 * hm]),
                     _pad_lanes(gs_full[..., 2 * hm:2 * hm + hgd]),
                     _pad_lanes(gs_full[..., 2 * hm + hgd:2 * hm + 2 * hgd])], axis=1)
    prm = jnp.pad(jnp.stack([_pad_lanes(v.astype(F32)) for v in (b_igate[0], b_fgate[0], a_log[0], dt_bias[0])]),
                  ((0, 4), (0, 0)))
    planes = _gates(gin, prm)

    colm = _pad_lanes(planes[:, 0:5, :, 0:hm].transpose(0, 2, 1, 3).reshape(b, s_full, 5 * hm))
    colg = planes[:, 5:10, :, 0:hgd].reshape(b, 5, s_full, groups, hg).transpose(0, 2, 3, 1, 4)
    colg = _pad_lanes(colg.reshape(b, s_full, groups, 5 * hg)).reshape(b, s_full, groups * LANES)
    erow = planes[:, _P_E, :, 0:hm].reshape(b, n_chunks + 1, CHUNK, hm).transpose(0, 1, 3, 2)
    grow = planes[:, _P_GC, :, 0:hgd].reshape(b, n_chunks + 1, CHUNK, hgd).transpose(0, 1, 3, 2)

    qkv_meta = _conv(p_meta, jnp.zeros((16, 3 * d), BF16), conv_w[0], b=1, s=CHUNK, d=d, dg=dg, col0=4 * d)
    qkv_main = _conv(p_main, p_meta[CHUNK - 16:, 4 * d:7 * d], conv_w[0], b=b, s=s, d=d, dg=dg, col0=4 * d)

    dqk = d // 2 // hm
    dv = d // hm
    c0 = jnp.zeros((hm, dqk, dv), F32)
    n0 = jnp.zeros((hm, 1, dqk), F32)
    _, c1, n1 = _mlstm(p_meta, colm[:1, :CHUNK], erow[:1, :1], c0, n0, ml_norm_g[0], b=1, s=CHUNK, d=d, hm=hm)
    ya, _, _ = _mlstm(p_main, colm[:, CHUNK:], erow[:, 1:], c1[0], n1[0], ml_norm_g[0], b=b, s=s, d=d, hm=hm)

    s0 = jnp.zeros((hgd, dg, dg), F32)
    _, s1 = _gdn(qkv_meta, p_meta, colg[:1, :CHUNK], grow[:1, :1], s0, gdn_norm_g[0],
                 b=1, s=CHUNK, d=d, hgd=hgd, hg=hg, zcol0=7 * d)
    yb, _ = _gdn(qkv_main, p_main, colg[:, CHUNK:], grow[:, 1:], s1[0], gdn_norm_g[0],
                 b=b, s=s, d=d, hgd=hgd, hg=hg, zcol0=7 * d)

    merged = _merge(ya, yb, wa, wb, p_main, gcol0=8 * d)
    pre = _outproj(merged, wo, x2d)
    return _rmsnorm(pre, norm_f_g, x.dtype).reshape(b, s, d)
```

```python
import functools

import jax
import jax.numpy as jnp
from jax import lax
from jax.experimental import pallas as pl
from jax.experimental.pallas import tpu as pltpu

CHUNK = 64
N_META = 16
N_PAD = CHUNK - N_META
EPS = 1e-6
NEG = -1e30
GATE_CAP = 15.0
LANES = 128
VMEM_LIMIT = 56 * 1024 * 1024
_CONV_HALO = 16
_WPREP_ROWS = 16

F32 = jnp.float32
BF16 = jnp.bfloat16

(_P_C, _P_WINTER, _P_FLOOR, _P_WKF, _P_DECAY, _P_GC, _P_BETA, _P_EXPGC, _P_KDECF, _P_EXPGL,
 _P_E) = range(11)
_N_PLANES = 11


def _sigmoid(x):
    return 1.0 / (1.0 + jnp.exp(-x))


def _softplus(x):
    return jnp.maximum(x, 0.0) + jnp.log1p(jnp.exp(-jnp.abs(x)))


def _pick(n, cands):
    for c in cands:
        if c <= n and n % c == 0:
            return c
    return n


def _params(sem, vmem=None):
    return pltpu.CompilerParams(dimension_semantics=sem, vmem_limit_bytes=vmem)


def _rmsnorm_kernel(x_ref, g_ref, o_ref):
    x = x_ref[...].astype(F32)
    ms = jnp.mean(x * x, axis=-1, keepdims=True)
    o_ref[...] = (x * lax.rsqrt(ms + EPS) * g_ref[...]).astype(o_ref.dtype)


def _rmsnorm(x2d, g, out_dtype):
    m, d = x2d.shape
    tr = _pick(m, (256, 128, 64))
    return pl.pallas_call(
        _rmsnorm_kernel,
        grid=(m // tr,),
        in_specs=[pl.BlockSpec((tr, d), lambda i: (i, 0)),
                  pl.BlockSpec((1, d), lambda i: (0, 0))],
        out_specs=pl.BlockSpec((tr, d), lambda i: (i, 0)),
        out_shape=jax.ShapeDtypeStruct((m, d), out_dtype),
        compiler_params=_params(("parallel",)),
        name="rmsnorm",
    )(x2d, g.reshape(1, d).astype(F32))


def _wprep_kernel(w_ref, o_ref, *, segs, tk):
    def body(s, carry):
        rows = pl.ds(pl.multiple_of(s * _WPREP_ROWS, _WPREP_ROWS), _WPREP_ROWS)
        dst = 0
        for src, width in segs:
            lo = src // LANES * LANES
            hi = min(-(-(src + width) // LANES) * LANES, w_ref.shape[1])
            x = w_ref[rows, lo:hi]
            o_ref[rows, dst:dst + width] = x[:, src - lo:src - lo + width].astype(o_ref.dtype)
            dst += width
        return carry

    lax.fori_loop(0, tk // _WPREP_ROWS, body, 0)


def _wprep(w, segs):
    k, n_in = w.shape
    n_out = sum(width for _, width in segs)
    tk = _pick(k, (64, 32, 16))
    return pl.pallas_call(
        functools.partial(_wprep_kernel, segs=segs, tk=tk),
        grid=(k // tk,),
        in_specs=[pl.BlockSpec((tk, n_in), lambda i: (i, 0))],
        out_specs=pl.BlockSpec((tk, n_out), lambda i: (i, 0)),
        out_shape=jax.ShapeDtypeStruct((k, n_out), BF16),
        compiler_params=_params(("parallel",), VMEM_LIMIT),
        name="wprep",
    )(w)


def _inproj_kernel(h_ref, w_ref, ws_ref, p_ref, gs_ref):
    h = h_ref[...]
    p_ref[...] = jnp.dot(h, w_ref[...], preferred_element_type=F32).astype(p_ref.dtype)

    @pl.when(pl.program_id(1) == 0)
    def _():
        gs_ref[...] = jnp.dot(h, ws_ref[...], preferred_element_type=F32)


def _inproj(hn, w_big, w_small):
    m, d = hn.shape
    n = w_big.shape[1]
    tm = _pick(m, (1024, 512, 256, 128, 64))
    tn = _pick(n, (1024, 512, 256, 128))
    return pl.pallas_call(
        _inproj_kernel,
        grid=(m // tm, n // tn),
        in_specs=[pl.BlockSpec((tm, d), lambda i, j: (i, 0)),
                  pl.BlockSpec((d, tn), lambda i, j: (0, j)),
                  pl.BlockSpec((d, LANES), lambda i, j: (0, 0))],
        out_specs=[pl.BlockSpec((tm, tn), lambda i, j: (i, j)),
                   pl.BlockSpec((tm, LANES), lambda i, j: (i, 0))],
        out_shape=[jax.ShapeDtypeStruct((m, n), BF16),
                   jax.ShapeDtypeStruct((m, LANES), F32)],
        compiler_params=_params(("parallel", "arbitrary"), VMEM_LIMIT),
        name="inproj",
    )(hn, w_big, w_small)


def _cumsum_rows(x, rowi):
    for d in (1, 2, 4, 8, 16, 32):
        x = x + jnp.where(rowi >= d, pltpu.roll(x, d, axis=0), 0.0)
    return x


def _cummax_rows(x, rowi):
    for d in (1, 2, 4, 8, 16, 32):
        x = jnp.maximum(x, jnp.where(rowi >= d, pltpu.roll(x, d, axis=0), NEG))
    return x


def _gates_kernel(gin_ref, prm_ref, out_ref, m_ref, *, tc):
    t = pl.program_id(1)

    @pl.when(t == 0)
    def _():
        m_ref[...] = jnp.zeros_like(m_ref)

    rowi = lax.broadcasted_iota(jnp.int32, (CHUNK, LANES), 0)
    b_i = prm_ref[0:1, :]
    b_f = prm_ref[1:2, :]
    neg_a = -jnp.exp(prm_ref[2:3, :])
    dtb = prm_ref[3:4, :]
    last = CHUNK - 1
    for ci in range(tc):
        rows = slice(ci * CHUNK, (ci + 1) * CHUNK)
        valid = ((t * tc + ci) * CHUNK + rowi) >= N_PAD
        m_i = gin_ref[0, 0, rows, :]
        m_f = gin_ref[0, 1, rows, :]
        g_a = gin_ref[0, 2, rows, :]
        g_b = gin_ref[0, 3, rows, :]

        i_pre = GATE_CAP * jnp.tanh((m_i + b_i) / GATE_CAP)
        f_pre = GATE_CAP * jnp.tanh((m_f + b_f) / GATE_CAP)
        i_pre = jnp.where(valid, i_pre, NEG)
        logf = jnp.where(valid, -_softplus(-f_pre), 0.0)
        bcum = _cumsum_rows(logf, rowi)
        e = i_pre - bcum
        cm = _cummax_rows(e, rowi)
        gtot = bcum[last:last + 1, :]
        m_old = m_ref[0:1, :]
        m_new = jnp.maximum(gtot + m_old, gtot + cm[last:last + 1, :])
        c = jnp.maximum(cm, m_old)
        out_ref[0, _P_C, rows, :] = c
        out_ref[0, _P_WINTER, rows, :] = jnp.exp(m_old - c)
        out_ref[0, _P_FLOOR, rows, :] = jnp.exp(-(bcum + c))
        out_ref[0, _P_WKF, rows, :] = jnp.exp(gtot + e - m_new)
        out_ref[0, _P_DECAY, rows, :] = jnp.broadcast_to(jnp.exp(gtot + m_old - m_new), (CHUNK, LANES))
        out_ref[0, _P_E, rows, :] = e
        m_ref[...] = jnp.broadcast_to(m_new, m_ref.shape)

        g = jnp.where(valid, neg_a * _softplus(g_a + dtb), 0.0)
        beta = jnp.where(valid, _sigmoid(g_b), 0.0)
        gc = _cumsum_rows(g, rowi)
        gl = gc[last:last + 1, :]
        out_ref[0, _P_GC, rows, :] = gc
        out_ref[0, _P_BETA, rows, :] = beta
        out_ref[0, _P_EXPGC, rows, :] = jnp.exp(gc)
        out_ref[0, _P_KDECF, rows, :] = jnp.exp(gl - gc)
        out_ref[0, _P_EXPGL, rows, :] = jnp.broadcast_to(jnp.exp(gl), (CHUNK, LANES))


def _gates(gin, prm):
    b, _, s_full, _ = gin.shape
    n_full = s_full // CHUNK
    tc = _pick(n_full, (3, 4, 2, 1))
    return pl.pallas_call(
        functools.partial(_gates_kernel, tc=tc),
        grid=(b, n_full // tc),
        in_specs=[pl.BlockSpec((1, 4, tc * CHUNK, LANES), lambda i, t: (i, 0, t, 0)),
                  pl.BlockSpec((8, LANES), lambda i, t: (0, 0))],
        out_specs=pl.BlockSpec((1, _N_PLANES, tc * CHUNK, LANES), lambda i, t: (i, 0, t, 0)),
        out_shape=jax.ShapeDtypeStruct((b, _N_PLANES, s_full, LANES), F32),
        scratch_shapes=[pltpu.VMEM((8, LANES), F32)],
        compiler_params=_params(("arbitrary", "arbitrary")),
        name="gates",
    )(gin, prm)


def _conv_kernel(x_ref, halo_ref, first_ref, w_ref, o_ref, ext_ref, *, r, cb, dg, ncb, scale):
    i = pl.program_id(1)
    cbi = pl.program_id(2)
    w = w_ref[...]
    is_v = cbi >= 2 * ncb
    qscale = jnp.where(cbi < ncb, scale, 1.0).astype(F32)
    hb = _CONV_HALO
    ext_ref[0:hb, :] = jnp.where(i == 0, first_ref[...], halo_ref[...]).astype(F32)

    for r0 in range(0, r, CHUNK):
        ext_ref[hb + r0:hb + r0 + CHUNK, :] = x_ref[r0:r0 + CHUNK, :].astype(F32)
        y = (w[3:4, :] * ext_ref[hb + r0:hb + r0 + CHUNK, :] + w[2:3, :] * ext_ref[hb - 1 + r0:hb - 1 + r0 + CHUNK, :]
             + w[1:2, :] * ext_ref[hb - 2 + r0:hb - 2 + r0 + CHUNK, :]
             + w[0:1, :] * ext_ref[hb - 3 + r0:hb - 3 + r0 + CHUNK, :])
        a = y * _sigmoid(y)
        for hd in range(cb // dg):
            seg = a[:, hd * dg:(hd + 1) * dg]
            ss = jnp.sum(seg * seg, axis=1, keepdims=True)
            fac = jnp.where(is_v, 1.0, lax.rsqrt(ss + EPS) * qscale)
            o_ref[r0:r0 + CHUNK, hd * dg:(hd + 1) * dg] = (seg * fac).astype(o_ref.dtype)


def _conv(p, first, conv_w, *, b, s, d, dg, col0):
    r = _pick(s, (512, 256, 128, 64))
    cb = _pick(d, (1024, 512, 256, 128))
    cb = max(cb, dg)
    ncb = d // cb
    nr = s // r
    cbase = col0 // cb
    hb = _CONV_HALO
    return pl.pallas_call(
        functools.partial(_conv_kernel, r=r, cb=cb, dg=dg, ncb=ncb, scale=float(dg) ** -0.5),
        grid=(b, nr, 3 * ncb),
        in_specs=[pl.BlockSpec((r, cb), lambda bi, i, c: (bi * nr + i, cbase + c)),
                  pl.BlockSpec((hb, cb), lambda bi, i, c: (jnp.maximum((bi * s + i * r) // hb - 1, 0), cbase + c)),
                  pl.BlockSpec((hb, cb), lambda bi, i, c: (0, c)),
                  pl.BlockSpec((4, cb), lambda bi, i, c: (0, c))],
        out_specs=pl.BlockSpec((r, cb), lambda bi, i, c: (bi * nr + i, c)),
        out_shape=jax.ShapeDtypeStruct((b * s, 3 * d), BF16),
        scratch_shapes=[pltpu.VMEM((hb + r, cb), F32)],
        compiler_params=_params(("parallel", "parallel", "parallel")),
        name="conv",
    )(p, p, first, conv_w.astype(F32))


def _mlstm_kernel(q_ref, k_ref, v_ref, o_ref, z_ref, colm_ref, erow_ref, c0_ref, n0_ref, g_ref,
                  ya_ref, c_ref, n_ref, *, t_chunks, hm, dqk, dv, scale):
    @pl.when(pl.program_id(1) == 0)
    def _():
        c_ref[...] = c0_ref[...]
        n_ref[...] = n0_ref[...]

    ri = lax.broadcasted_iota(jnp.int32, (CHUNK, CHUNK), 0)
    ci = lax.broadcasted_iota(jnp.int32, (CHUNK, CHUNK), 1)
    causal = ci <= ri

    def chunk_body(c, carry):
        r0 = pl.multiple_of(c * CHUNK, CHUNK)
        rows = pl.ds(r0, CHUNK)
        colm = colm_ref[0, rows, :]
        for h in range(hm):
            qs = slice(h * dqk, (h + 1) * dqk)
            vs = slice(h * dv, (h + 1) * dv)
            q = q_ref[rows, qs]
            k = k_ref[rows, qs]
            v = v_ref[rows, vs]
            cc = colm[:, _P_C * hm + h:_P_C * hm + h + 1]
            winter = colm[:, _P_WINTER * hm + h:_P_WINTER * hm + h + 1]
            floor = colm[:, _P_FLOOR * hm + h:_P_FLOOR * hm + h + 1]
            wkf = colm[:, _P_WKF * hm + h:_P_WKF * hm + h + 1]
            decay = colm[0:1, _P_DECAY * hm + h:_P_DECAY * hm + h + 1]
            e_row = erow_ref[0, c, h:h + 1, :]

            qk = lax.dot_general(q, k, (((1,), (1,)), ((), ())), preferred_element_type=F32) * scale
            s = qk * jnp.exp(jnp.where(causal, e_row - cc, NEG))
            den_intra = jnp.sum(s, axis=1, keepdims=True)
            c_st = c_ref[h]
            n_st = n_ref[h]
            num = winter * jnp.dot(q, c_st.astype(BF16), preferred_element_type=F32) + jnp.dot(
                s.astype(BF16), v, preferred_element_type=F32)
            qn = jnp.sum(q.astype(F32) * n_st, axis=1, keepdims=True)
            den = winter * qn + den_intra
            hh = num * (1.0 / jnp.maximum(jnp.abs(den), floor))
            ms = jnp.mean(hh * hh, axis=1, keepdims=True)
            hn = hh * lax.rsqrt(ms + EPS) * g_ref[:, vs]
            og = o_ref[rows, vs].astype(F32)
            zg = z_ref[rows, vs].astype(F32)
            ya_ref[rows, vs] = (hn * _sigmoid(og) * (zg * _sigmoid(zg))).astype(ya_ref.dtype)

            wk = k.astype(F32) * (wkf * scale)
            upd = lax.dot_general(wk.astype(BF16), v, (((0,), (0,)), ((), ())), preferred_element_type=F32)
            c_ref[h] = decay * c_st + upd
            n_ref[h] = decay * n_st + jnp.sum(wk, axis=0, keepdims=True)
        return carry

    lax.fori_loop(0, t_chunks, chunk_body, 0)


def _mlstm(p, colm, erow, c0, n0, g, *, b, s, d, hm):
    dqk = d // 2 // hm
    dv = d // hm
    t_chunks = _pick(s // CHUNK, (4, 2, 1))
    rb = t_chunks * CHUNK
    nt = s // rb
    hw = d // 2
    row = lambda bi, t: bi * nt + t
    return pl.pallas_call(
        functools.partial(_mlstm_kernel, t_chunks=t_chunks, hm=hm, dqk=dqk, dv=dv, scale=float(dqk) ** -0.5),
        grid=(b, nt),
        in_specs=[pl.BlockSpec((rb, hw), lambda bi, t: (row(bi, t), 0)),
                  pl.BlockSpec((rb, hw), lambda bi, t: (row(bi, t), 1)),
                  pl.BlockSpec((rb, d), lambda bi, t: (row(bi, t), 1)),
                  pl.BlockSpec((rb, d), lambda bi, t: (row(bi, t), 2)),
                  pl.BlockSpec((rb, d), lambda bi, t: (row(bi, t), 3)),
                  pl.BlockSpec((1, rb, LANES), lambda bi, t: (bi, t, 0)),
                  pl.BlockSpec((1, t_chunks, hm, CHUNK), lambda bi, t: (bi, t, 0, 0)),
                  pl.BlockSpec((hm, dqk, dv), lambda bi, t: (0, 0, 0)),
                  pl.BlockSpec((hm, 1, dqk), lambda bi, t: (0, 0, 0)),
                  pl.BlockSpec((1, d), lambda bi, t: (0, 0))],
        out_specs=[pl.BlockSpec((rb, d), lambda bi, t: (row(bi, t), 0)),
                   pl.BlockSpec((None, hm, dqk, dv), lambda bi, t: (bi, 0, 0, 0)),
                   pl.BlockSpec((None, hm, 1, dqk), lambda bi, t: (bi, 0, 0, 0))],
        out_shape=[jax.ShapeDtypeStruct((b * s, d), BF16),
                   jax.ShapeDtypeStruct((b, hm, dqk, dv), F32),
                   jax.ShapeDtypeStruct((b, hm, 1, dqk), F32)],
        compiler_params=_params(("arbitrary", "arbitrary"), VMEM_LIMIT),
        name="mlstm",
    )(p, p, p, p, p, colm, erow, c0, n0, g.reshape(1, d).astype(F32))


def _gdn_kernel(q_ref, k_ref, v_ref, z_ref, colg_ref, grow_ref, s0_ref, g_ref, yb_ref, s_ref,
                *, t_chunks, hg, dg):
    @pl.when(pl.program_id(2) == 0)
    def _():
        s_ref[...] = s0_ref[...]

    ri = lax.broadcasted_iota(jnp.int32, (CHUNK, CHUNK), 0)
    ci = lax.broadcasted_iota(jnp.int32, (CHUNK, CHUNK), 1)
    incl = ci <= ri
    strict = ci < ri
    eye = (ci == ri).astype(F32)
    tdims = (((0,), (0,)), ((), ()))
    ndims = (((1,), (1,)), ((), ()))

    def chunk_body(c, carry):
        r0 = pl.multiple_of(c * CHUNK, CHUNK)
        rows = pl.ds(r0, CHUNK)
        colg = colg_ref[0, rows, :]
        hs = range(hg)
        sls = [slice(h * dg, (h + 1) * dg) for h in hs]
        col = lambda j, h: colg[:, j * hg + h:j * hg + h + 1]
        q = [q_ref[rows, sls[h]] for h in hs]
        k = [k_ref[rows, sls[h]] for h in hs]
        v = [v_ref[rows, sls[h]] for h in hs]
        dec = [jnp.exp(jnp.where(incl, col(0, h) - grow_ref[0, c, h:h + 1, :], NEG)) for h in hs]
        kk = [lax.dot_general(k[h], k[h], ndims, preferred_element_type=F32) for h in hs]
        qk = [lax.dot_general(q[h], k[h], ndims, preferred_element_type=F32) * dec[h] for h in hs]
        a = [jnp.where(strict, kk[h] * dec[h], 0.0) * col(1, h) for h in hs]
        pw = a
        tm = [eye - a[h] for h in hs]
        for _ in range(5):
            pb = [pw[h].astype(BF16) for h in hs]
            pw = [jnp.dot(pb[h], pb[h], preferred_element_type=F32) for h in hs]
            tm = [tm[h] + jnp.dot(tm[h].astype(BF16), pw[h].astype(BF16), preferred_element_type=F32) for h in hs]
        kf = [k[h].astype(F32) for h in hs]
        rhs = [jnp.concatenate([v[h].astype(F32) * col(1, h), kf[h] * (col(1, h) * col(2, h))], axis=1).astype(BF16)
               for h in hs]
        uw = [jnp.dot(tm[h].astype(BF16), rhs[h], preferred_element_type=F32) for h in hs]
        s_st = [s_ref[h] for h in hs]
        s_b = [s_st[h].astype(BF16) for h in hs]
        v_new = [uw[h][:, :dg] - jnp.dot(uw[h][:, dg:].astype(BF16), s_b[h], preferred_element_type=F32) for h in hs]
        v_nb = [v_new[h].astype(BF16) for h in hs]
        q_dec = [(q[h].astype(F32) * col(2, h)).astype(BF16) for h in hs]
        o = [jnp.dot(q_dec[h], s_b[h], preferred_element_type=F32)
             + jnp.dot(qk[h].astype(BF16), v_nb[h], preferred_element_type=F32) for h in hs]
        k_dec = [(kf[h] * col(3, h)).astype(BF16) for h in hs]
        for h in hs:
            expgl = colg[0:1, 4 * hg + h:4 * hg + h + 1]
            s_ref[h] = expgl * s_st[h] + lax.dot_general(k_dec[h], v_nb[h], tdims, preferred_element_type=F32)
        for h in hs:
            ms = jnp.mean(o[h] * o[h], axis=1, keepdims=True)
            zg = z_ref[rows, sls[h]].astype(F32)
            yb_ref[rows, sls[h]] = (o[h] * lax.rsqrt(ms + EPS) * g_ref[...] * (zg * _sigmoid(zg))).astype(yb_ref.dtype)
        return carry

    lax.fori_loop(0, t_chunks, chunk_body, 0)


def _gdn(qkv, p, colg, grow, s0, g, *, b, s, d, hgd, hg, zcol0):
    dg = d // hgd
    groups = hgd // hg
    gw = hg * dg
    t_chunks = _pick(s // CHUNK, (4, 2, 1))
    rb = t_chunks * CHUNK
    nt = s // rb
    zb = zcol0 // gw
    row = lambda bi, t: bi * nt + t
    return pl.pallas_call(
        functools.partial(_gdn_kernel, t_chunks=t_chunks, hg=hg, dg=dg),
        grid=(b, groups, nt),
        in_specs=[pl.BlockSpec((rb, gw), lambda bi, gi, t: (row(bi, t), gi)),
                  pl.BlockSpec((rb, gw), lambda bi, gi, t: (row(bi, t), groups + gi)),
                  pl.BlockSpec((rb, gw), lambda bi, gi, t: (row(bi, t), 2 * groups + gi)),
                  pl.BlockSpec((rb, gw), lambda bi, gi, t: (row(bi, t), zb + gi)),
                  pl.BlockSpec((1, rb, LANES), lambda bi, gi, t: (bi, t, gi)),
                  pl.BlockSpec((1, t_chunks, hg, CHUNK), lambda bi, gi, t: (bi, t, gi, 0)),
                  pl.BlockSpec((hg, dg, dg), lambda bi, gi, t: (gi, 0, 0)),
                  pl.BlockSpec((1, dg), lambda bi, gi, t: (0, 0))],
        out_specs=[pl.BlockSpec((rb, gw), lambda bi, gi, t: (row(bi, t), gi)),
                   pl.BlockSpec((None, hg, dg, dg), lambda bi, gi, t: (bi, gi, 0, 0))],
        out_shape=[jax.ShapeDtypeStruct((b * s, d), BF16),
                   jax.ShapeDtypeStruct((b, hgd, dg, dg), F32)],
        compiler_params=_params(("arbitrary", "arbitrary", "arbitrary"), VMEM_LIMIT),
        name="gdn",
    )(qkv, qkv, qkv, p, colg, grow, s0, g.reshape(1, dg).astype(F32))


def _merge_kernel(ya_ref, yb_ref, wa_ref, wb_ref, ga_ref, gb_ref, o_ref):
    pa = jnp.dot(ya_ref[...], wa_ref[...], preferred_element_type=F32)
    pb = jnp.dot(yb_ref[...], wb_ref[...], preferred_element_type=F32)
    o_ref[...] = (_sigmoid(ga_ref[...].astype(F32)) * pa + _sigmoid(gb_ref[...].astype(F32)) * pb).astype(o_ref.dtype)


def _merge(ya, yb, wa, wb, p, *, gcol0):
    m, d = ya.shape
    tm = _pick(m, (256, 128, 64))
    tn = _pick(d, (1024, 512, 256, 128))
    ga0 = gcol0 // tn
    gb0 = (gcol0 + d) // tn
    return pl.pallas_call(
        _merge_kernel,
        grid=(d // tn, m // tm),
        in_specs=[pl.BlockSpec((tm, d), lambda j, i: (i, 0)),
                  pl.BlockSpec((tm, d), lambda j, i: (i, 0)),
                  pl.BlockSpec((d, tn), lambda j, i: (0, j)),
                  pl.BlockSpec((d, tn), lambda j, i: (0, j)),
                  pl.BlockSpec((tm, tn), lambda j, i: (i, ga0 + j)),
                  pl.BlockSpec((tm, tn), lambda j, i: (i, gb0 + j))],
        out_specs=pl.BlockSpec((tm, tn), lambda j, i: (i, j)),
        out_shape=jax.ShapeDtypeStruct((m, d), BF16),
        compiler_params=_params(("parallel", "parallel"), VMEM_LIMIT),
        name="merge",
    )(ya, yb, wa, wb, p, p)


def _outproj_kernel(m_ref, w_ref, x_ref, o_ref):
    o_ref[...] = x_ref[...] + jnp.dot(m_ref[...], w_ref[...], preferred_element_type=F32)


def _outproj(merged, wo, x2d):
    m, d = merged.shape
    tm = _pick(m, (1024, 512, 256, 128, 64))
    tn = _pick(d, (512, 256, 128))
    return pl.pallas_call(
        _outproj_kernel,
        grid=(m // tm, d // tn),
        in_specs=[pl.BlockSpec((tm, d), lambda i, j: (i, 0)),
                  pl.BlockSpec((d, tn), lambda i, j: (0, j)),
                  pl.BlockSpec((tm, tn), lambda i, j: (i, j))],
        out_specs=pl.BlockSpec((tm, tn), lambda i, j: (i, j)),
        out_shape=jax.ShapeDtypeStruct((m, d), F32),
        compiler_params=_params(("parallel", "parallel"), VMEM_LIMIT),
        name="outproj",
    )(merged, wo, x2d)


def _pad_lanes(a, width=LANES):
    return jnp.pad(a, [(0, 0)] * (a.ndim - 1) + [(0, width - a.shape[-1])])


def kernel(x, meta, norm_in_g, w_in, b_igate, b_fgate, ml_norm_g, conv_w, a_log, dt_bias,
           gdn_norm_g, w_proj_a, w_proj_b, w_out, norm_f_g):
    assert norm_in_g.shape[0] == 1, "single-layer block"
    b, s, d = x.shape
    hm = b_igate.shape[-1]
    hgd = a_log.shape[-1]
    dg = d // hgd
    hg = min(8, hgd)
    groups = hgd // hg
    assert s % CHUNK == 0 and meta.shape[0] == N_META and 2 * hm + 2 * hgd <= LANES
    n_chunks = s // CHUNK
    s_full = s + CHUNK

    w = w_in[0]
    g0 = 4 * d
    g1 = g0 + 2 * hm
    g2 = g1 + 4 * d
    g3 = g2 + 2 * hgd
    w_big = _wprep(w, ((0, g0), (g1, g2 - g1), (g3, 2 * d)))
    w_small = _pad_lanes(jnp.concatenate([w[:, g0:g1], w[:, g2:g3]], axis=1)).astype(BF16)
    wa = w_proj_a[0].astype(BF16)
    wb = w_proj_b[0].astype(BF16)
    wo = w_out[0].astype(BF16)

    x2d = x.reshape(b * s, d)
    h0 = jnp.concatenate([jnp.zeros((N_PAD, d), x.dtype), meta.astype(x.dtype)], axis=0)
    p_main, gs_main = _inproj(_rmsnorm(x2d, norm_in_g[0], BF16), w_big, w_small)
    p_meta, gs_meta = _inproj(_rmsnorm(h0, norm_in_g[0], BF16), w_big, w_small)

    gs_full = jnp.concatenate([jnp.broadcast_to(gs_meta[None], (b, CHUNK, LANES)),
                               gs_main.reshape(b, s, LANES)], axis=1)
    gin = jnp.stack([_pad_lanes(gs_full[..., 0:hm]), _pad_lanes(gs_full[..., hm:2 * hm]),
                     _pad_lanes(gs_full[..., 2 * hm:2 * hm + hgd]),
                     _pad_lanes(gs_full[..., 2 * hm + hgd:2 * hm + 2 * hgd])], axis=1)
    prm = jnp.pad(jnp.stack([_pad_lanes(v.astype(F32)) for v in (b_igate[0], b_fgate[0], a_log[0], dt_bias[0])]),
                  ((0, 4), (0, 0)))
    planes = _gates(gin, prm)

    colm = _pad_lanes(planes[:, 0:5, :, 0:hm].transpose(0, 2, 1, 3).reshape(b, s_full, 5 * hm))
    colg = planes[:, 5:10, :, 0:hgd].reshape(b, 5, s_full, groups, hg).transpose(0, 2, 3, 1, 4)
    colg = _pad_lanes(colg.reshape(b, s_full, groups, 5 * hg)).reshape(b, s_full, groups * LANES)
    erow = planes[:, _P_E, :, 0:hm].reshape(b, n_chunks + 1, CHUNK, hm).transpose(0, 1, 3, 2)
    grow = planes[:, _P_GC, :, 0:hgd].reshape(b, n_chunks + 1, CHUNK, hgd).transpose(0, 1, 3, 2)

    qkv_meta = _conv(p_meta, jnp.zeros((_CONV_HALO, 3 * d), BF16), conv_w[0], b=1, s=CHUNK, d=d, dg=dg, col0=4 * d)
    qkv_main = _conv(p_main, p_meta[CHUNK - _CONV_HALO:, 4 * d:7 * d], conv_w[0], b=b, s=s, d=d, dg=dg, col0=4 * d)

    dqk = d // 2 // hm
    dv = d // hm
    c0 = jnp.zeros((hm, dqk, dv), F32)
    n0 = jnp.zeros((hm, 1, dqk), F32)
    _, c1, n1 = _mlstm(p_meta, colm[:1, :CHUNK], erow[:1, :1], c0, n0, ml_norm_g[0], b=1, s=CHUNK, d=d, hm=hm)
    ya, _, _ = _mlstm(p_main, colm[:, CHUNK:], erow[:, 1:], c1[0], n1[0], ml_norm_g[0], b=b, s=s, d=d, hm=hm)

    s0 = jnp.zeros((hgd, dg, dg), F32)
    _, s1 = _gdn(qkv_meta, p_meta, colg[:1, :CHUNK], grow[:1, :1], s0, gdn_norm_g[0],
                 b=1, s=CHUNK, d=d, hgd=hgd, hg=hg, zcol0=7 * d)
    yb, _ = _gdn(qkv_main, p_main, colg[:, CHUNK:], grow[:, 1:], s1[0], gdn_norm_g[0],
                 b=b, s=s, d=d, hgd=hgd, hg=hg, zcol0=7 * d)

    merged = _merge(ya, yb, wa, wb, p_main, gcol0=8 * d)
    pre = _outproj(merged, wo, x2d)
    return _rmsnorm(pre, norm_f_g, x.dtype).reshape(b, s, d)
```

```python
import functools

import jax
import jax.numpy as jnp
from jax import lax
from jax.experimental import pallas as pl
from jax.experimental.pallas import tpu as pltpu

CHUNK = 64
N_META = 16
N_PAD = CHUNK - N_META
EPS = 1e-6
NEG = -1e30
GATE_CAP = 15.0
LANES = 128
VMEM_LIMIT = 56 * 1024 * 1024
_CONV_HALO = 16
_MXU_TILE = 256
_PACK = _MXU_TILE // CHUNK

F32 = jnp.float32
BF16 = jnp.bfloat16

(_P_C, _P_WINTER, _P_FLOOR, _P_WKF, _P_DECAY, _P_GC, _P_BETA, _P_EXPGC, _P_KDECF, _P_EXPGL,
 _P_E) = range(11)
_N_PLANES = 11


def _sigmoid(x):
    return 1.0 / (1.0 + jnp.exp(-x))


def _softplus(x):
    return jnp.maximum(x, 0.0) + jnp.log1p(jnp.exp(-jnp.abs(x)))


def _pick(n, cands):
    for c in cands:
        if c <= n and n % c == 0:
            return c
    return n


def _params(sem, vmem=None):
    return pltpu.CompilerParams(dimension_semantics=sem, vmem_limit_bytes=vmem)


def _rmsnorm_kernel(x_ref, g_ref, o_ref):
    x = x_ref[...].astype(F32)
    ms = jnp.mean(x * x, axis=-1, keepdims=True)
    o_ref[...] = (x * lax.rsqrt(ms + EPS) * g_ref[...]).astype(o_ref.dtype)


def _rmsnorm(x2d, g, out_dtype):
    m, d = x2d.shape
    tr = _pick(m, (256, 128, 64))
    return pl.pallas_call(
        _rmsnorm_kernel,
        grid=(m // tr,),
        in_specs=[pl.BlockSpec((tr, d), lambda i: (i, 0)),
                  pl.BlockSpec((1, d), lambda i: (0, 0))],
        out_specs=pl.BlockSpec((tr, d), lambda i: (i, 0)),
        out_shape=jax.ShapeDtypeStruct((m, d), out_dtype),
        compiler_params=_params(("parallel",)),
        name="rmsnorm",
    )(x2d, g.reshape(1, d).astype(F32))


_NT = (((1,), (1,)), ((), ()))


def _inproj_kernel(h_ref, hm_ref, wt_hbm, p_ref, pm_ref, wstage, wbf, sem, *, tn, nj, seg_tiles, seg_skip):
    j = pl.program_id(0)
    i = pl.program_id(1)

    def tile_copy(jj):
        off = jj * tn
        for tiles, skip in zip(seg_tiles, seg_skip):
            off = off + jnp.where(jj >= tiles, skip, 0)
        return pltpu.make_async_copy(wt_hbm.at[pl.ds(pl.multiple_of(off, 8), tn), :], wstage, sem)

    @pl.when(i == 0)
    def _():
        @pl.when(j == 0)
        def _():
            tile_copy(j).start()

        tile_copy(j).wait()

        def cast_rows(s, carry):
            rows = pl.ds(pl.multiple_of(s * CHUNK, CHUNK), CHUNK)
            wbf[rows, :] = wstage[rows, :].astype(wbf.dtype)
            return carry

        lax.fori_loop(0, tn // CHUNK, cast_rows, 0)

        @pl.when(j + 1 < nj)
        def _():
            tile_copy(j + 1).start()

        pm_ref[...] = lax.dot_general(hm_ref[...], wbf[...], _NT, preferred_element_type=F32).astype(pm_ref.dtype)

    p_ref[...] = lax.dot_general(h_ref[...], wbf[...], _NT, preferred_element_type=F32).astype(p_ref.dtype)


def _inproj(hn, hn_meta, wt, *, n_out, seg_starts, seg_skip):
    m, d = hn.shape
    mm = hn_meta.shape[0]
    tm = _pick(m, (1024, 512, 256, 128, 64))
    tn = _pick(n_out, (1024, 512, 256, 128))
    assert all(st % tn == 0 for st in seg_starts) and all(sk % 8 == 0 for sk in seg_skip)
    nj = n_out // tn
    return pl.pallas_call(
        functools.partial(_inproj_kernel, tn=tn, nj=nj, seg_tiles=tuple(st // tn for st in seg_starts),
                          seg_skip=tuple(seg_skip)),
        grid=(nj, m // tm),
        in_specs=[pl.BlockSpec((tm, d), lambda j, i: (i, 0)),
                  pl.BlockSpec((mm, d), lambda j, i: (0, 0)),
                  pl.BlockSpec(memory_space=pl.ANY)],
        out_specs=[pl.BlockSpec((tm, tn), lambda j, i: (i, j)),
                   pl.BlockSpec((mm, tn), lambda j, i: (0, j))],
        out_shape=[jax.ShapeDtypeStruct((m, n_out), BF16),
                   jax.ShapeDtypeStruct((mm, n_out), BF16)],
        scratch_shapes=[pltpu.VMEM((tn, d), F32), pltpu.VMEM((tn, d), BF16), pltpu.SemaphoreType.DMA(())],
        compiler_params=_params(("arbitrary", "arbitrary"), VMEM_LIMIT),
        name="inproj",
    )(hn, hn_meta, wt)


def _gateproj_kernel(h_ref, ws_ref, o_ref):
    o_ref[...] = lax.dot_general(h_ref[...], ws_ref[...].astype(h_ref.dtype), _NT, preferred_element_type=F32)


def _gateproj(hn, ws):
    m, d = hn.shape
    tm = _pick(m, (1024, 512, 256, 128, 64))
    return pl.pallas_call(
        _gateproj_kernel,
        grid=(m // tm,),
        in_specs=[pl.BlockSpec((tm, d), lambda i: (i, 0)),
                  pl.BlockSpec((LANES, d), lambda i: (0, 0))],
        out_specs=pl.BlockSpec((tm, LANES), lambda i: (i, 0)),
        out_shape=jax.ShapeDtypeStruct((m, LANES), F32),
        compiler_params=_params(("parallel",)),
        name="gateproj",
    )(hn, ws)


def _cumsum_rows(x, rowi):
    for d in (1, 2, 4, 8, 16, 32):
        x = x + jnp.where(rowi >= d, pltpu.roll(x, d, axis=0), 0.0)
    return x


def _cummax_rows(x, rowi):
    for d in (1, 2, 4, 8, 16, 32):
        x = jnp.maximum(x, jnp.where(rowi >= d, pltpu.roll(x, d, axis=0), NEG))
    return x


def _gates_kernel(gin_ref, prm_ref, out_ref, m_ref, *, tc):
    t = pl.program_id(1)

    @pl.when(t == 0)
    def _():
        m_ref[...] = jnp.zeros_like(m_ref)

    rowi = lax.broadcasted_iota(jnp.int32, (CHUNK, LANES), 0)
    b_i = prm_ref[0:1, :]
    b_f = prm_ref[1:2, :]
    neg_a = -jnp.exp(prm_ref[2:3, :])
    dtb = prm_ref[3:4, :]
    last = CHUNK - 1
    for ci in range(tc):
        rows = slice(ci * CHUNK, (ci + 1) * CHUNK)
        valid = ((t * tc + ci) * CHUNK + rowi) >= N_PAD
        m_i = gin_ref[0, 0, rows, :]
        m_f = gin_ref[0, 1, rows, :]
        g_a = gin_ref[0, 2, rows, :]
        g_b = gin_ref[0, 3, rows, :]

        i_pre = GATE_CAP * jnp.tanh((m_i + b_i) / GATE_CAP)
        f_pre = GATE_CAP * jnp.tanh((m_f + b_f) / GATE_CAP)
        i_pre = jnp.where(valid, i_pre, NEG)
        logf = jnp.where(valid, -_softplus(-f_pre), 0.0)
        bcum = _cumsum_rows(logf, rowi)
        e = i_pre - bcum
        cm = _cummax_rows(e, rowi)
        gtot = bcum[last:last + 1, :]
        m_old = m_ref[0:1, :]
        m_new = jnp.maximum(gtot + m_old, gtot + cm[last:last + 1, :])
        c = jnp.maximum(cm, m_old)
        out_ref[0, _P_C, rows, :] = c
        out_ref[0, _P_WINTER, rows, :] = jnp.exp(m_old - c)
        out_ref[0, _P_FLOOR, rows, :] = jnp.exp(-(bcum + c))
        out_ref[0, _P_WKF, rows, :] = jnp.exp(gtot + e - m_new)
        out_ref[0, _P_DECAY, rows, :] = jnp.broadcast_to(jnp.exp(gtot + m_old - m_new), (CHUNK, LANES))
        out_ref[0, _P_E, rows, :] = e
        m_ref[...] = jnp.broadcast_to(m_new, m_ref.shape)

        g = jnp.where(valid, neg_a * _softplus(g_a + dtb), 0.0)
        beta = jnp.where(valid, _sigmoid(g_b), 0.0)
        gc = _cumsum_rows(g, rowi)
        gl = gc[last:last + 1, :]
        out_ref[0, _P_GC, rows, :] = gc
        out_ref[0, _P_BETA, rows, :] = beta
        out_ref[0, _P_EXPGC, rows, :] = jnp.exp(gc)
        out_ref[0, _P_KDECF, rows, :] = jnp.exp(gl - gc)
        out_ref[0, _P_EXPGL, rows, :] = jnp.broadcast_to(jnp.exp(gl), (CHUNK, LANES))


def _gates(gin, prm):
    b, _, s_full, _ = gin.shape
    n_full = s_full // CHUNK
    tc = _pick(n_full, (3, 4, 2, 1))
    return pl.pallas_call(
        functools.partial(_gates_kernel, tc=tc),
        grid=(b, n_full // tc),
        in_specs=[pl.BlockSpec((1, 4, tc * CHUNK, LANES), lambda i, t: (i, 0, t, 0)),
                  pl.BlockSpec((8, LANES), lambda i, t: (0, 0))],
        out_specs=pl.BlockSpec((1, _N_PLANES, tc * CHUNK, LANES), lambda i, t: (i, 0, t, 0)),
        out_shape=jax.ShapeDtypeStruct((b, _N_PLANES, s_full, LANES), F32),
        scratch_shapes=[pltpu.VMEM((8, LANES), F32)],
        compiler_params=_params(("arbitrary", "arbitrary")),
        name="gates",
    )(gin, prm)


def _conv_kernel(x_ref, halo_ref, first_ref, w_ref, o_ref, *, r, cb, dg, ncb, scale):
    i = pl.program_id(1)
    cbi = pl.program_id(2)
    w = w_ref[...]
    is_v = cbi >= 2 * ncb
    qscale = jnp.where(cbi < ncb, scale, 1.0).astype(F32)
    hb = _CONV_HALO
    halo = jnp.where(i == 0, first_ref[...], halo_ref[...])
    ntap = w_ref.shape[0]
    sel_r = lax.broadcasted_iota(jnp.int32, (ntap * CHUNK, hb + CHUNK), 0)
    sel_c = lax.broadcasted_iota(jnp.int32, (ntap * CHUNK, hb + CHUNK), 1)
    select = (sel_c == hb - sel_r // CHUNK + sel_r % CHUNK).astype(x_ref.dtype)

    for r0 in range(0, r, CHUNK):
        if r0 == 0:
            ext = jnp.concatenate([halo, x_ref[0:CHUNK, :]], axis=0)
        else:
            ext = x_ref[r0 - hb:r0 + CHUNK, :]
        taps = jnp.dot(select, ext, preferred_element_type=F32)
        y = w[ntap - 1:ntap, :] * taps[0:CHUNK]
        for tap in range(1, ntap):
            y = y + w[ntap - 1 - tap:ntap - tap, :] * taps[tap * CHUNK:(tap + 1) * CHUNK]
        a = y * _sigmoid(y)
        for hd in range(cb // dg):
            seg = a[:, hd * dg:(hd + 1) * dg]
            ss = jnp.sum(seg * seg, axis=1, keepdims=True)
            fac = jnp.where(is_v, 1.0, lax.rsqrt(ss + EPS) * qscale)
            o_ref[r0:r0 + CHUNK, hd * dg:(hd + 1) * dg] = (seg * fac).astype(o_ref.dtype)


def _conv(p, first, conv_w, *, b, s, d, dg, col0):
    r = _pick(s, (512, 256, 128, 64))
    cb = _pick(d, (1024, 512, 256, 128))
    cb = max(cb, dg)
    ncb = d // cb
    nr = s // r
    cbase = col0 // cb
    hb = _CONV_HALO
    return pl.pallas_call(
        functools.partial(_conv_kernel, r=r, cb=cb, dg=dg, ncb=ncb, scale=float(dg) ** -0.5),
        grid=(b, nr, 3 * ncb),
        in_specs=[pl.BlockSpec((r, cb), lambda bi, i, c: (bi * nr + i, cbase + c)),
                  pl.BlockSpec((hb, cb), lambda bi, i, c: (jnp.maximum((bi * s + i * r) // hb - 1, 0), cbase + c)),
                  pl.BlockSpec((hb, cb), lambda bi, i, c: (0, c)),
                  pl.BlockSpec((4, cb), lambda bi, i, c: (0, c))],
        out_specs=pl.BlockSpec((r, cb), lambda bi, i, c: (bi * nr + i, c)),
        out_shape=jax.ShapeDtypeStruct((b * s, 3 * d), BF16),
        compiler_params=_params(("parallel", "parallel", "parallel")),
        name="conv",
    )(p, p, first, conv_w.astype(F32))


def _mlstm_kernel(q_ref, k_ref, v_ref, o_ref, z_ref, colm_ref, erow_ref, c0_ref, n0_ref, g_ref,
                  ya_ref, c_ref, n_ref, *, t_chunks, hm, dqk, dv, scale):
    @pl.when(pl.program_id(1) == 0)
    def _():
        c_ref[...] = c0_ref[...]
        n_ref[...] = n0_ref[...]

    ri = lax.broadcasted_iota(jnp.int32, (CHUNK, CHUNK), 0)
    ci = lax.broadcasted_iota(jnp.int32, (CHUNK, CHUNK), 1)
    causal = ci <= ri

    def chunk_body(c, carry):
        r0 = pl.multiple_of(c * CHUNK, CHUNK)
        rows = pl.ds(r0, CHUNK)
        colm = colm_ref[0, rows, :]
        for h in range(hm):
            qs = slice(h * dqk, (h + 1) * dqk)
            vs = slice(h * dv, (h + 1) * dv)
            q = q_ref[rows, qs]
            k = k_ref[rows, qs]
            v = v_ref[rows, vs]
            cc = colm[:, _P_C * hm + h:_P_C * hm + h + 1]
            winter = colm[:, _P_WINTER * hm + h:_P_WINTER * hm + h + 1]
            floor = colm[:, _P_FLOOR * hm + h:_P_FLOOR * hm + h + 1]
            wkf = colm[:, _P_WKF * hm + h:_P_WKF * hm + h + 1]
            decay = colm[0:1, _P_DECAY * hm + h:_P_DECAY * hm + h + 1]
            e_row = erow_ref[0, c, h:h + 1, :]

            qk = lax.dot_general(q, k, (((1,), (1,)), ((), ())), preferred_element_type=F32) * scale
            s = qk * jnp.exp(jnp.where(causal, e_row - cc, NEG))
            den_intra = jnp.sum(s, axis=1, keepdims=True)
            c_st = c_ref[h]
            n_st = n_ref[h]
            num = winter * jnp.dot(q, c_st.astype(BF16), preferred_element_type=F32) + jnp.dot(
                s.astype(BF16), v, preferred_element_type=F32)
            qn = jnp.sum(q.astype(F32) * n_st, axis=1, keepdims=True)
            den = winter * qn + den_intra
            hh = num * (1.0 / jnp.maximum(jnp.abs(den), floor))
            ms = jnp.mean(hh * hh, axis=1, keepdims=True)
            hn = hh * lax.rsqrt(ms + EPS) * g_ref[:, vs]
            og = o_ref[rows, vs].astype(F32)
            zg = z_ref[rows, vs].astype(F32)
            ya_ref[rows, vs] = (hn * _sigmoid(og) * (zg * _sigmoid(zg))).astype(ya_ref.dtype)

            wk = k.astype(F32) * (wkf * scale)
            upd = lax.dot_general(wk.astype(BF16), v, (((0,), (0,)), ((), ())), preferred_element_type=F32)
            c_ref[h] = decay * c_st + upd
            n_ref[h] = decay * n_st + jnp.sum(wk, axis=0, keepdims=True)
        return carry

    lax.fori_loop(0, t_chunks, chunk_body, 0)


def _mlstm(p, colm, erow, c0, n0, g, *, b, s, d, hm):
    dqk = d // 2 // hm
    dv = d // hm
    t_chunks = _pick(s // CHUNK, (4, 2, 1))
    rb = t_chunks * CHUNK
    nt = s // rb
    hw = d // 2
    row = lambda bi, t: bi * nt + t
    return pl.pallas_call(
        functools.partial(_mlstm_kernel, t_chunks=t_chunks, hm=hm, dqk=dqk, dv=dv, scale=float(dqk) ** -0.5),
        grid=(b, nt),
        in_specs=[pl.BlockSpec((rb, hw), lambda bi, t: (row(bi, t), 0)),
                  pl.BlockSpec((rb, hw), lambda bi, t: (row(bi, t), 1)),
                  pl.BlockSpec((rb, d), lambda bi, t: (row(bi, t), 1)),
                  pl.BlockSpec((rb, d), lambda bi, t: (row(bi, t), 2)),
                  pl.BlockSpec((rb, d), lambda bi, t: (row(bi, t), 3)),
                  pl.BlockSpec((1, rb, LANES), lambda bi, t: (bi, t, 0)),
                  pl.BlockSpec((1, t_chunks, hm, CHUNK), lambda bi, t: (bi, t, 0, 0)),
                  pl.BlockSpec((hm, dqk, dv), lambda bi, t: (0, 0, 0)),
                  pl.BlockSpec((hm, 1, dqk), lambda bi, t: (0, 0, 0)),
                  pl.BlockSpec((1, d), lambda bi, t: (0, 0))],
        out_specs=[pl.BlockSpec((rb, d), lambda bi, t: (row(bi, t), 0)),
                   pl.BlockSpec((None, hm, dqk, dv), lambda bi, t: (bi, 0, 0, 0)),
                   pl.BlockSpec((None, hm, 1, dqk), lambda bi, t: (bi, 0, 0, 0))],
        out_shape=[jax.ShapeDtypeStruct((b * s, d), BF16),
                   jax.ShapeDtypeStruct((b, hm, dqk, dv), F32),
                   jax.ShapeDtypeStruct((b, hm, 1, dqk), F32)],
        compiler_params=_params(("arbitrary", "arbitrary"), VMEM_LIMIT),
        name="mlstm",
    )(p, p, p, p, p, colm, erow, c0, n0, g.reshape(1, d).astype(F32))


def _gdn_kernel(q_ref, k_ref, v_ref, z_ref, colg_ref, grow_ref, s0_ref, g_ref, yb_ref, s_ref, qk_scr, u_scr, wq_scr,
                *, t_chunks, hg, dg):
    @pl.when(pl.program_id(2) == 0)
    def _():
        s_ref[...] = s0_ref[...]

    npk = hg // _PACK
    side = _PACK * CHUNK
    row = lax.broadcasted_iota(jnp.int32, (CHUNK, side), 0)
    lane = lax.broadcasted_iota(jnp.int32, (CHUNK, side), 1)
    lj = lane % CHUNK
    lgrp = lane // CHUNK
    incl = lj <= row
    strict = lj < row
    eye = (lj == row).astype(F32)
    bd_mask = (lax.broadcasted_iota(jnp.int32, (side, side), 0) // CHUNK
               == lax.broadcasted_iota(jnp.int32, (side, side), 1) // CHUNK)
    kgrp = lax.broadcasted_iota(jnp.int32, (CHUNK, _PACK * dg), 1) // dg
    tdims = (((0,), (0,)), ((), ()))

    def blockdiag(x):
        return jnp.where(bd_mask, jnp.concatenate([x] * _PACK, axis=0), jnp.zeros((), x.dtype))

    def spread(colg, j, g):
        base = j * hg + g * _PACK
        acc = jnp.broadcast_to(colg[:, base:base + 1], (CHUNK, side))
        for hq in range(1, _PACK):
            acc = jnp.where(lgrp == hq, colg[:, base + hq:base + hq + 1], acc)
        return acc

    units = [(c, g) for c in range(t_chunks) for g in range(npk)]
    us = range(len(units))
    rws = [slice(c * CHUNK, (c + 1) * CHUNK) for c, _ in units]
    lns = [slice(g * _PACK * dg, (g + 1) * _PACK * dg) for _, g in units]
    colgs = [colg_ref[0, rws[u], :] for u in us]
    k4 = [k_ref[rws[u], lns[u]] for u in us]
    q4 = [q_ref[rws[u], lns[u]] for u in us]
    kbd = [jnp.concatenate([jnp.where(kgrp == hq, k4[u], jnp.zeros((), k4[u].dtype)) for hq in range(_PACK)], axis=0)
           for u in us]
    qkkk = [lax.dot_general(jnp.concatenate([q4[u], k4[u]], axis=0), kbd[u], _NT, preferred_element_type=F32)
            for u in us]
    dec = [jnp.exp(jnp.where(incl, spread(colgs[u], 0, units[u][1])
                             - grow_ref[0, units[u][0], 0, 0, units[u][1]:units[u][1] + 1, :], NEG)) for u in us]
    for u in us:
        qk_scr[units[u][0], units[u][1]] = (qkkk[u][:CHUNK] * dec[u]).astype(qk_scr.dtype)
    a = [jnp.where(strict, qkkk[u][CHUNK:] * dec[u], 0.0) * spread(colgs[u], 1, units[u][1]) for u in us]
    tm = [eye - a[u] for u in us]
    ab = [a[u].astype(BF16) for u in us]
    pw = [jnp.dot(ab[u], blockdiag(ab[u]), preferred_element_type=F32) for u in us]
    for _ in range(4):
        pb = [pw[u].astype(BF16) for u in us]
        both = [jnp.dot(jnp.concatenate([tm[u].astype(BF16), pb[u]], axis=0), blockdiag(pb[u]),
                        preferred_element_type=F32) for u in us]
        tm = [tm[u] + both[u][:CHUNK] for u in us]
        pw = [both[u][CHUNK:] for u in us]
    tb = [((tm[u] + jnp.dot(tm[u].astype(BF16), blockdiag(pw[u].astype(BF16)), preferred_element_type=F32))
           * grow_ref[0, units[u][0], 0, 1, units[u][1]:units[u][1] + 1, :]).astype(BF16) for u in us]

    def pad_rows(x, hq):
        parts = []
        if hq:
            parts.append(jnp.zeros((hq * CHUNK, x.shape[1]), x.dtype))
        parts.append(x)
        if hq < _PACK - 1:
            parts.append(jnp.zeros(((_PACK - 1 - hq) * CHUNK, x.shape[1]), x.dtype))
        return jnp.concatenate(parts, axis=0)

    items = [(u, hq) for u in us for hq in range(_PACK)]
    heads = [units[u][1] * _PACK + hq for u, hq in items]
    hsl = [slice(h * dg, (h + 1) * dg) for h in heads]
    expgc = [colgs[u][:, 2 * hg + h:2 * hg + h + 1] for (u, _), h in zip(items, heads)]
    rhs = [jnp.concatenate([v_ref[rws[u], hsl[i]], (k_ref[rws[u], hsl[i]].astype(F32) * expgc[i]).astype(BF16)], axis=1)
           for i, (u, _) in enumerate(items)]
    uw = [jnp.dot(tb[u], pad_rows(rhs[i], hq), preferred_element_type=F32) for i, (u, hq) in enumerate(items)]
    for i, (u, _) in enumerate(items):
        c, h = units[u][0], heads[i]
        u_scr[c, h] = uw[i][:, :dg]
        wq_scr[c, h, 0:CHUNK, :] = uw[i][:, dg:].astype(wq_scr.dtype)
        wq_scr[c, h, CHUNK:2 * CHUNK, :] = (q_ref[rws[u], hsl[i]].astype(F32) * expgc[i]).astype(wq_scr.dtype)

    def chunk_body(c, carry):
        r0 = pl.multiple_of(c * CHUNK, CHUNK)
        rows = pl.ds(r0, CHUNK)
        colg = colg_ref[0, rows, :]
        hs = range(hg)
        sls = [slice(h * dg, (h + 1) * dg) for h in hs]
        col = lambda j, h: colg[:, j * hg + h:j * hg + h + 1]
        s_st = [s_ref[h] for h in hs]
        s_b = [s_st[h].astype(BF16) for h in hs]
        wqs = [jnp.dot(wq_scr[c, h], s_b[h], preferred_element_type=F32) for h in hs]
        v_nb = [(u_scr[c, h] - wqs[h][:CHUNK]).astype(BF16) for h in hs]
        o = [wqs[h][CHUNK:] + jnp.dot(qk_scr[c, h // _PACK], pad_rows(v_nb[h], h % _PACK), preferred_element_type=F32)
             for h in hs]
        k_dec = [(k_ref[rows, sls[h]].astype(F32) * col(3, h)).astype(BF16) for h in hs]
        for h in hs:
            expgl = colg[0:1, 4 * hg + h:4 * hg + h + 1]
            s_ref[h] = expgl * s_st[h] + lax.dot_general(k_dec[h], v_nb[h], tdims, preferred_element_type=F32)
        for h in hs:
            ms = jnp.mean(o[h] * o[h], axis=1, keepdims=True)
            zg = z_ref[rows, sls[h]].astype(F32)
            yb_ref[rows, sls[h]] = (o[h] * lax.rsqrt(ms + EPS) * g_ref[...] * (zg * _sigmoid(zg))).astype(yb_ref.dtype)
        return carry

    lax.fori_loop(0, t_chunks, chunk_body, 0, unroll=2 if t_chunks % 2 == 0 else 1)


def _gdn(qkv, p, colg, grow, s0, g, *, b, s, d, hgd, hg, zcol0):
    dg = d // hgd
    groups = hgd // hg
    gw = hg * dg
    npk = hg // _PACK
    side = _PACK * CHUNK
    t_chunks = _pick(s // CHUNK, (8, 4, 2, 1))
    rb = t_chunks * CHUNK
    nt = s // rb
    zb = zcol0 // gw
    row = lambda bi, t: bi * nt + t
    return pl.pallas_call(
        functools.partial(_gdn_kernel, t_chunks=t_chunks, hg=hg, dg=dg),
        grid=(b, groups, nt),
        in_specs=[pl.BlockSpec((rb, gw), lambda bi, gi, t: (row(bi, t), gi)),
                  pl.BlockSpec((rb, gw), lambda bi, gi, t: (row(bi, t), groups + gi)),
                  pl.BlockSpec((rb, gw), lambda bi, gi, t: (row(bi, t), 2 * groups + gi)),
                  pl.BlockSpec((rb, gw), lambda bi, gi, t: (row(bi, t), zb + gi)),
                  pl.BlockSpec((1, rb, LANES), lambda bi, gi, t: (bi, t, gi)),
                  pl.BlockSpec((1, t_chunks, 1, 2, npk, side), lambda bi, gi, t: (bi, t, gi, 0, 0, 0)),
                  pl.BlockSpec((hg, dg, dg), lambda bi, gi, t: (gi, 0, 0)),
                  pl.BlockSpec((1, dg), lambda bi, gi, t: (0, 0))],
        out_specs=[pl.BlockSpec((rb, gw), lambda bi, gi, t: (row(bi, t), gi)),
                   pl.BlockSpec((None, hg, dg, dg), lambda bi, gi, t: (bi, gi, 0, 0))],
        out_shape=[jax.ShapeDtypeStruct((b * s, d), BF16),
                   jax.ShapeDtypeStruct((b, hgd, dg, dg), F32)],
        scratch_shapes=[pltpu.VMEM((t_chunks, npk, CHUNK, side), BF16),
                        pltpu.VMEM((t_chunks, hg, CHUNK, dg), F32),
                        pltpu.VMEM((t_chunks, hg, 2 * CHUNK, dg), BF16)],
        compiler_params=_params(("arbitrary", "arbitrary", "arbitrary"), VMEM_LIMIT),
        name="gdn",
    )(qkv, qkv, qkv, p, colg, grow, s0, g.reshape(1, dg).astype(F32))


def _merge_kernel(ya_ref, yb_ref, wa_ref, wb_ref, ga_ref, gb_ref, o_ref):
    pa = jnp.dot(ya_ref[...], wa_ref[...], preferred_element_type=F32)
    pb = jnp.dot(yb_ref[...], wb_ref[...], preferred_element_type=F32)
    o_ref[...] = (_sigmoid(ga_ref[...].astype(F32)) * pa + _sigmoid(gb_ref[...].astype(F32)) * pb).astype(o_ref.dtype)


def _merge(ya, yb, wa, wb, p, *, gcol0):
    m, d = ya.shape
    tm = _pick(m, (256, 128, 64))
    tn = _pick(d, (1024, 512, 256, 128))
    ga0 = gcol0 // tn
    gb0 = (gcol0 + d) // tn
    return pl.pallas_call(
        _merge_kernel,
        grid=(d // tn, m // tm),
        in_specs=[pl.BlockSpec((tm, d), lambda j, i: (i, 0)),
                  pl.BlockSpec((tm, d), lambda j, i: (i, 0)),
                  pl.BlockSpec((d, tn), lambda j, i: (0, j)),
                  pl.BlockSpec((d, tn), lambda j, i: (0, j)),
                  pl.BlockSpec((tm, tn), lambda j, i: (i, ga0 + j)),
                  pl.BlockSpec((tm, tn), lambda j, i: (i, gb0 + j))],
        out_specs=pl.BlockSpec((tm, tn), lambda j, i: (i, j)),
        out_shape=jax.ShapeDtypeStruct((m, d), BF16),
        compiler_params=_params(("parallel", "parallel"), VMEM_LIMIT),
        name="merge",
    )(ya, yb, wa, wb, p, p)


def _outproj_kernel(m_ref, w_ref, x_ref, g_ref, o_ref, ss_ref, *, tn, nj, d):
    j = pl.program_id(1)
    y = x_ref[...] + jnp.dot(m_ref[...], w_ref[...], preferred_element_type=F32)
    part = jnp.sum(y * y, axis=1, keepdims=True)

    @pl.when(j == 0)
    def _():
        ss_ref[...] = part

    @pl.when(j > 0)
    def _():
        ss_ref[...] += part

    for jj in range(nj):
        @pl.when(j == jj)
        def _(jj=jj):
            o_ref[:, jj * tn:(jj + 1) * tn] = y

    @pl.when(j == nj - 1)
    def _():
        o_ref[...] = o_ref[...] * lax.rsqrt(ss_ref[...] / d + EPS) * g_ref[...]


def _outproj(merged, wo, x2d, g):
    m, d = merged.shape
    tm = _pick(m, (512, 256, 128, 64))
    tn = _pick(d, (512, 256, 128))
    nj = d // tn
    return pl.pallas_call(
        functools.partial(_outproj_kernel, tn=tn, nj=nj, d=d),
        grid=(m // tm, nj),
        in_specs=[pl.BlockSpec((tm, d), lambda i, j: (i, 0)),
                  pl.BlockSpec((d, tn), lambda i, j: (0, j)),
                  pl.BlockSpec((tm, tn), lambda i, j: (i, j)),
                  pl.BlockSpec((1, d), lambda i, j: (0, 0))],
        out_specs=pl.BlockSpec((tm, d), lambda i, j: (i, 0)),
        out_shape=jax.ShapeDtypeStruct((m, d), F32),
        scratch_shapes=[pltpu.VMEM((tm, 1), F32)],
        compiler_params=_params(("parallel", "arbitrary"), VMEM_LIMIT),
        name="outproj",
    )(merged, wo, x2d, g.reshape(1, d).astype(F32))


def _pad_lanes(a, width=LANES):
    return jnp.pad(a, [(0, 0)] * (a.ndim - 1) + [(0, width - a.shape[-1])])


def kernel(x, meta, norm_in_g, w_in, b_igate, b_fgate, ml_norm_g, conv_w, a_log, dt_bias,
           gdn_norm_g, w_proj_a, w_proj_b, w_out, norm_f_g):
    assert norm_in_g.shape[0] == 1, "single-layer block"
    b, s, d = x.shape
    hm = b_igate.shape[-1]
    hgd = a_log.shape[-1]
    dg = d // hgd
    hg = min(8, hgd)
    groups = hgd // hg
    assert s % CHUNK == 0 and meta.shape[0] == N_META and 2 * hm + 2 * hgd <= LANES
    n_chunks = s // CHUNK
    s_full = s + CHUNK

    wt = jnp.swapaxes(w_in[0], 0, 1)
    g0 = 4 * d
    g1 = g0 + 2 * hm
    g2 = g1 + 4 * d
    g3 = g2 + 2 * hgd
    w_gate = jnp.pad(jnp.concatenate([wt[g0:g1], wt[g2:g3]], axis=0), ((0, LANES - 2 * hm - 2 * hgd), (0, 0)))
    wa = w_proj_a[0].astype(BF16)
    wb = w_proj_b[0].astype(BF16)
    wo = w_out[0].astype(BF16)

    x2d = x.reshape(b * s, d)
    h0 = jnp.concatenate([jnp.zeros((N_PAD, d), x.dtype), meta.astype(x.dtype)], axis=0)
    hn_main = _rmsnorm(x2d, norm_in_g[0], BF16)
    hn_meta = _rmsnorm(h0, norm_in_g[0], BF16)
    p_main, p_meta = _inproj(hn_main, hn_meta, wt, n_out=10 * d, seg_starts=(4 * d, 8 * d),
                             seg_skip=(2 * hm, 2 * hgd))
    gs_main = _gateproj(hn_main, w_gate)
    gs_meta = _gateproj(hn_meta, w_gate)

    gs_full = jnp.concatenate([jnp.broadcast_to(gs_meta[None], (b, CHUNK, LANES)),
                               gs_main.reshape(b, s, LANES)], axis=1)
    gin = jnp.stack([_pad_lanes(gs_full[..., 0:hm]), _pad_lanes(gs_full[..., hm:2 * hm]),
                     _pad_lanes(gs_full[..., 2 * hm:2 * hm + hgd]),
                     _pad_lanes(gs_full[..., 2 * hm + hgd:2 * hm + 2 * hgd])], axis=1)
    prm = jnp.pad(jnp.stack([_pad_lanes(v.astype(F32)) for v in (b_igate[0], b_fgate[0], a_log[0], dt_bias[0])]),
                  ((0, 4), (0, 0)))
    planes = _gates(gin, prm)

    colm = _pad_lanes(planes[:, 0:5, :, 0:hm].transpose(0, 2, 1, 3).reshape(b, s_full, 5 * hm))
    colg = planes[:, 5:10, :, 0:hgd].reshape(b, 5, s_full, groups, hg).transpose(0, 2, 3, 1, 4)
    colg = _pad_lanes(colg.reshape(b, s_full, groups, 5 * hg)).reshape(b, s_full, groups * LANES)
    erow = planes[:, _P_E, :, 0:hm].reshape(b, n_chunks + 1, CHUNK, hm).transpose(0, 1, 3, 2)
    grow = planes[:, _P_GC:_P_BETA + 1, :, 0:hgd].reshape(b, 2, n_chunks + 1, CHUNK, groups, hg)
    grow = grow.transpose(0, 2, 4, 1, 5, 3).reshape(b, n_chunks + 1, groups, 2, hg // _PACK, _PACK * CHUNK)

    qkv_meta = _conv(p_meta, jnp.zeros((_CONV_HALO, 3 * d), BF16), conv_w[0], b=1, s=CHUNK, d=d, dg=dg, col0=4 * d)
    qkv_main = _conv(p_main, p_meta[CHUNK - _CONV_HALO:, 4 * d:7 * d], conv_w[0], b=b, s=s, d=d, dg=dg, col0=4 * d)

    dqk = d // 2 // hm
    dv = d // hm
    c0 = jnp.zeros((hm, dqk, dv), F32)
    n0 = jnp.zeros((hm, 1, dqk), F32)
    _, c1, n1 = _mlstm(p_meta, colm[:1, :CHUNK], erow[:1, :1], c0, n0, ml_norm_g[0], b=1, s=CHUNK, d=d, hm=hm)
    ya, _, _ = _mlstm(p_main, colm[:, CHUNK:], erow[:, 1:], c1[0], n1[0], ml_norm_g[0], b=b, s=s, d=d, hm=hm)

    s0 = jnp.zeros((hgd, dg, dg), F32)
    _, s1 = _gdn(qkv_meta, p_meta, colg[:1, :CHUNK], grow[:1, :1], s0, gdn_norm_g[0],
                 b=1, s=CHUNK, d=d, hgd=hgd, hg=hg, zcol0=7 * d)
    yb, _ = _gdn(qkv_main, p_main, colg[:, CHUNK:], grow[:, 1:], s1[0], gdn_norm_g[0],
                 b=b, s=s, d=d, hgd=hgd, hg=hg, zcol0=7 * d)

    merged = _merge(ya, yb, wa, wb, p_main, gcol0=8 * d)
    return _outproj(merged, wo, x2d, norm_f_g).reshape(b, s, d)
```

```python
import functools

import jax
import jax.numpy as jnp
from jax import lax
from jax.experimental import pallas as pl
from jax.experimental.pallas import tpu as pltpu

CHUNK = 64
N_META = 16
N_PAD = CHUNK - N_META
EPS = 1e-6
NEG = -1e30
GATE_CAP = 15.0
LANES = 128
SUBLANES = 8
VMEM_LIMIT = 56 * 1024 * 1024
_CONV_SPLIT = 2
_MXU_TILE = 256
_PACK = _MXU_TILE // CHUNK

F32 = jnp.float32
BF16 = jnp.bfloat16

(_P_C, _P_WINTER, _P_FLOOR, _P_WKF, _P_DECAY, _P_GC, _P_BETA, _P_EXPGC, _P_KDECF, _P_EXPGL,
 _P_E) = range(11)
_N_PLANES = 11


def _sigmoid(x):
    return 1.0 / (1.0 + jnp.exp(-x))


def _softplus(x):
    return jnp.maximum(x, 0.0) + jnp.log1p(jnp.exp(-jnp.abs(x)))


def _pick(n, cands):
    for c in cands:
        if c <= n and n % c == 0:
            return c
    return n


def _params(sem, vmem=None):
    return pltpu.CompilerParams(dimension_semantics=sem, vmem_limit_bytes=vmem)


def _rmsnorm_kernel(x_ref, g_ref, o_ref):
    x = x_ref[...].astype(F32)
    ms = jnp.mean(x * x, axis=-1, keepdims=True)
    o_ref[...] = (x * lax.rsqrt(ms + EPS) * g_ref[...]).astype(o_ref.dtype)


def _rmsnorm(x2d, g, out_dtype):
    m, d = x2d.shape
    tr = _pick(m, (256, 128, 64))
    return pl.pallas_call(
        _rmsnorm_kernel,
        grid=(m // tr,),
        in_specs=[pl.BlockSpec((tr, d), lambda i: (i, 0)),
                  pl.BlockSpec((1, d), lambda i: (0, 0))],
        out_specs=pl.BlockSpec((tr, d), lambda i: (i, 0)),
        out_shape=jax.ShapeDtypeStruct((m, d), out_dtype),
        compiler_params=_params(("parallel",)),
        name="rmsnorm",
    )(x2d, g.reshape(1, d).astype(F32))


_NT = (((1,), (1,)), ((), ()))


def _inproj_kernel(h_ref, hm_ref, cw_ref, wt_hbm, p_ref, pm_ref, wstage, wbf, tail_scr, mtail_scr, sem,
                   *, tn, nj, seg_tiles, seg_skip, conv_j0, conv_nj, seq_tiles, dg, scale):
    j = pl.program_id(0)
    i = pl.program_id(1)
    hr = tail_scr.shape[0]
    is_conv = jnp.logical_and(j >= conv_j0, j < conv_j0 + 3 * conv_nj)
    is_v = j >= conv_j0 + 2 * conv_nj
    qscale = jnp.where(j < conv_j0 + conv_nj, scale, 1.0).astype(F32)

    def tile_copy(jj):
        off = jj * tn
        for tiles, skip in zip(seg_tiles, seg_skip):
            off = off + jnp.where(jj >= tiles, skip, 0)
        return pltpu.make_async_copy(wt_hbm.at[pl.ds(pl.multiple_of(off, 8), tn), :], wstage, sem)

    def conv_store(o_ref, acc, halo):
        m = acc.shape[0]
        w = cw_ref[...]
        ntap = w.shape[0]
        ext = jnp.concatenate([halo, acc], axis=0)
        y = w[ntap - 1:ntap, :] * acc
        for tap in range(1, ntap):
            y = y + w[ntap - 1 - tap:ntap - tap, :] * ext[hr - tap:hr - tap + m]
        a = y * _sigmoid(y)
        for hd in range(tn // dg):
            seg = a[:, hd * dg:(hd + 1) * dg]
            ss = jnp.sum(seg * seg, axis=1, keepdims=True)
            fac = jnp.where(is_v, 1.0, lax.rsqrt(ss + EPS) * qscale)
            o_ref[:, hd * dg:(hd + 1) * dg] = (seg * fac).astype(o_ref.dtype)

    @pl.when(i == 0)
    def _():
        @pl.when(j == 0)
        def _():
            tile_copy(j).start()

        tile_copy(j).wait()

        def cast_rows(s, carry):
            rows = pl.ds(pl.multiple_of(s * CHUNK, CHUNK), CHUNK)
            wbf[rows, :] = wstage[rows, :].astype(wbf.dtype)
            return carry

        lax.fori_loop(0, tn // CHUNK, cast_rows, 0)

        @pl.when(j + 1 < nj)
        def _():
            tile_copy(j + 1).start()

        accm = lax.dot_general(hm_ref[...], wbf[...], _NT, preferred_element_type=F32)
        mm = accm.shape[0]

        @pl.when(is_conv)
        def _():
            mtail_scr[...] = accm[mm - hr:mm]
            conv_store(pm_ref, accm, jnp.zeros((hr, tn), F32))

        @pl.when(jnp.logical_not(is_conv))
        def _():
            pm_ref[...] = accm.astype(pm_ref.dtype)

    @pl.when(is_conv)
    def _():
        tm = h_ref.shape[0]
        sub = tm // _CONV_SPLIT
        halo = jnp.where(i % seq_tiles == 0, mtail_scr[...], tail_scr[...])
        accs = []
        for part in range(_CONV_SPLIT):
            accs.append(lax.dot_general(h_ref[part * sub:(part + 1) * sub, :], wbf[...], _NT,
                                        preferred_element_type=F32))
            if part:
                conv_store(p_ref.at[(part - 1) * sub:part * sub, :], accs[part - 1], halo)
                halo = accs[part - 1][sub - hr:]
        conv_store(p_ref.at[tm - sub:tm, :], accs[-1], halo)
        tail_scr[...] = accs[-1][sub - hr:]

    @pl.when(jnp.logical_not(is_conv))
    def _():
        p_ref[...] = lax.dot_general(h_ref[...], wbf[...], _NT, preferred_element_type=F32).astype(p_ref.dtype)


def _inproj(hn, hn_meta, wt, conv_w, *, n_out, seg_starts, seg_skip, conv_col0, seq_rows, dg):
    m, d = hn.shape
    mm = hn_meta.shape[0]
    tm = _pick(seq_rows, (1024, 512, 256, 128, 64))
    tn = _pick(n_out, (1024, 512, 256, 128))
    assert all(st % tn == 0 for st in seg_starts) and all(sk % 8 == 0 for sk in seg_skip)
    assert m % tm == 0 and conv_col0 % tn == 0 and d % tn == 0 and tn % dg == 0 and seq_rows % tm == 0
    nj = n_out // tn
    conv_j0, conv_nj = conv_col0 // tn, d // tn
    return pl.pallas_call(
        functools.partial(_inproj_kernel, tn=tn, nj=nj, seg_tiles=tuple(st // tn for st in seg_starts),
                          seg_skip=tuple(seg_skip), conv_j0=conv_j0, conv_nj=conv_nj, seq_tiles=seq_rows // tm,
                          dg=dg, scale=float(dg) ** -0.5),
        grid=(nj, m // tm),
        in_specs=[pl.BlockSpec((tm, d), lambda j, i: (i, 0)),
                  pl.BlockSpec((mm, d), lambda j, i: (0, 0)),
                  pl.BlockSpec((conv_w.shape[0], tn), lambda j, i: (0, jnp.clip(j - conv_j0, 0, 3 * conv_nj - 1))),
                  pl.BlockSpec(memory_space=pl.ANY)],
        out_specs=[pl.BlockSpec((tm, tn), lambda j, i: (i, j)),
                   pl.BlockSpec((mm, tn), lambda j, i: (0, j))],
        out_shape=[jax.ShapeDtypeStruct((m, n_out), BF16),
                   jax.ShapeDtypeStruct((mm, n_out), BF16)],
        scratch_shapes=[pltpu.VMEM((tn, d), F32), pltpu.VMEM((tn, d), BF16),
                        pltpu.VMEM((SUBLANES, tn), F32), pltpu.VMEM((SUBLANES, tn), F32),
                        pltpu.SemaphoreType.DMA(())],
        compiler_params=_params(("arbitrary", "arbitrary"), VMEM_LIMIT),
        name="inproj",
    )(hn, hn_meta, conv_w.astype(F32), wt)


def _gateproj_kernel(h_ref, ws_ref, o_ref):
    o_ref[...] = lax.dot_general(h_ref[...], ws_ref[...].astype(h_ref.dtype), _NT, preferred_element_type=F32)


def _gateproj(hn, ws):
    m, d = hn.shape
    tm = _pick(m, (1024, 512, 256, 128, 64))
    return pl.pallas_call(
        _gateproj_kernel,
        grid=(m // tm,),
        in_specs=[pl.BlockSpec((tm, d), lambda i: (i, 0)),
                  pl.BlockSpec((LANES, d), lambda i: (0, 0))],
        out_specs=pl.BlockSpec((tm, LANES), lambda i: (i, 0)),
        out_shape=jax.ShapeDtypeStruct((m, LANES), F32),
        compiler_params=_params(("parallel",)),
        name="gateproj",
    )(hn, ws)


def _cumsum_rows(x, rowi):
    for d in (1, 2, 4, 8, 16, 32):
        x = x + jnp.where(rowi >= d, pltpu.roll(x, d, axis=0), 0.0)
    return x


def _cummax_rows(x, rowi):
    for d in (1, 2, 4, 8, 16, 32):
        x = jnp.maximum(x, jnp.where(rowi >= d, pltpu.roll(x, d, axis=0), NEG))
    return x


def _gates_kernel(gin_ref, prm_ref, out_ref, m_ref, *, tc):
    t = pl.program_id(1)

    @pl.when(t == 0)
    def _():
        m_ref[...] = jnp.zeros_like(m_ref)

    rowi = lax.broadcasted_iota(jnp.int32, (CHUNK, LANES), 0)
    b_i = prm_ref[0:1, :]
    b_f = prm_ref[1:2, :]
    neg_a = -jnp.exp(prm_ref[2:3, :])
    dtb = prm_ref[3:4, :]
    last = CHUNK - 1
    for ci in range(tc):
        rows = slice(ci * CHUNK, (ci + 1) * CHUNK)
        valid = ((t * tc + ci) * CHUNK + rowi) >= N_PAD
        m_i = gin_ref[0, 0, rows, :]
        m_f = gin_ref[0, 1, rows, :]
        g_a = gin_ref[0, 2, rows, :]
        g_b = gin_ref[0, 3, rows, :]

        i_pre = GATE_CAP * jnp.tanh((m_i + b_i) / GATE_CAP)
        f_pre = GATE_CAP * jnp.tanh((m_f + b_f) / GATE_CAP)
        i_pre = jnp.where(valid, i_pre, NEG)
        logf = jnp.where(valid, -_softplus(-f_pre), 0.0)
        bcum = _cumsum_rows(logf, rowi)
        e = i_pre - bcum
        cm = _cummax_rows(e, rowi)
        gtot = bcum[last:last + 1, :]
        m_old = m_ref[0:1, :]
        m_new = jnp.maximum(gtot + m_old, gtot + cm[last:last + 1, :])
        c = jnp.maximum(cm, m_old)
        out_ref[0, _P_C, rows, :] = c
        out_ref[0, _P_WINTER, rows, :] = jnp.exp(m_old - c)
        out_ref[0, _P_FLOOR, rows, :] = jnp.exp(-(bcum + c))
        out_ref[0, _P_WKF, rows, :] = jnp.exp(gtot + e - m_new)
        out_ref[0, _P_DECAY, rows, :] = jnp.broadcast_to(jnp.exp(gtot + m_old - m_new), (CHUNK, LANES))
        out_ref[0, _P_E, rows, :] = e
        m_ref[...] = jnp.broadcast_to(m_new, m_ref.shape)

        g = jnp.where(valid, neg_a * _softplus(g_a + dtb), 0.0)
        beta = jnp.where(valid, _sigmoid(g_b), 0.0)
        gc = _cumsum_rows(g, rowi)
        gl = gc[last:last + 1, :]
        out_ref[0, _P_GC, rows, :] = gc
        out_ref[0, _P_BETA, rows, :] = beta
        out_ref[0, _P_EXPGC, rows, :] = jnp.exp(gc)
        out_ref[0, _P_KDECF, rows, :] = jnp.exp(gl - gc)
        out_ref[0, _P_EXPGL, rows, :] = jnp.broadcast_to(jnp.exp(gl), (CHUNK, LANES))


def _gates(gin, prm):
    b, _, s_full, _ = gin.shape
    n_full = s_full // CHUNK
    tc = _pick(n_full, (3, 4, 2, 1))
    return pl.pallas_call(
        functools.partial(_gates_kernel, tc=tc),
        grid=(b, n_full // tc),
        in_specs=[pl.BlockSpec((1, 4, tc * CHUNK, LANES), lambda i, t: (i, 0, t, 0)),
                  pl.BlockSpec((8, LANES), lambda i, t: (0, 0))],
        out_specs=pl.BlockSpec((1, _N_PLANES, tc * CHUNK, LANES), lambda i, t: (i, 0, t, 0)),
        out_shape=jax.ShapeDtypeStruct((b, _N_PLANES, s_full, LANES), F32),
        scratch_shapes=[pltpu.VMEM((8, LANES), F32)],
        compiler_params=_params(("arbitrary", "arbitrary")),
        name="gates",
    )(gin, prm)


def _mlstm_kernel(q_ref, k_ref, v_ref, o_ref, z_ref, colm_ref, erow_ref, c0_ref, n0_ref, g_ref,
                  ya_ref, c_ref, n_ref, *, t_chunks, hm, dqk, dv, scale):
    @pl.when(pl.program_id(1) == 0)
    def _():
        c_ref[...] = c0_ref[...]
        n_ref[...] = n0_ref[...]

    ri = lax.broadcasted_iota(jnp.int32, (CHUNK, CHUNK), 0)
    ci = lax.broadcasted_iota(jnp.int32, (CHUNK, CHUNK), 1)
    causal = ci <= ri

    def chunk_body(c, carry):
        r0 = pl.multiple_of(c * CHUNK, CHUNK)
        rows = pl.ds(r0, CHUNK)
        colm = colm_ref[0, rows, :]
        for h in range(hm):
            qs = slice(h * dqk, (h + 1) * dqk)
            vs = slice(h * dv, (h + 1) * dv)
            q = q_ref[rows, qs]
            k = k_ref[rows, qs]
            v = v_ref[rows, vs]
            cc = colm[:, _P_C * hm + h:_P_C * hm + h + 1]
            winter = colm[:, _P_WINTER * hm + h:_P_WINTER * hm + h + 1]
            floor = colm[:, _P_FLOOR * hm + h:_P_FLOOR * hm + h + 1]
            wkf = colm[:, _P_WKF * hm + h:_P_WKF * hm + h + 1]
            decay = colm[0:1, _P_DECAY * hm + h:_P_DECAY * hm + h + 1]
            e_row = erow_ref[0, c, h:h + 1, :]

            qk = lax.dot_general(q, k, (((1,), (1,)), ((), ())), preferred_element_type=F32) * scale
            s = qk * jnp.exp(jnp.where(causal, e_row - cc, NEG))
            den_intra = jnp.sum(s, axis=1, keepdims=True)
            c_st = c_ref[h]
            n_st = n_ref[h]
            num = winter * jnp.dot(q, c_st.astype(BF16), preferred_element_type=F32) + jnp.dot(
                s.astype(BF16), v, preferred_element_type=F32)
            qn = jnp.sum(q.astype(F32) * n_st, axis=1, keepdims=True)
            den = winter * qn + den_intra
            hh = num * (1.0 / jnp.maximum(jnp.abs(den), floor))
            ms = jnp.mean(hh * hh, axis=1, keepdims=True)
            hn = hh * lax.rsqrt(ms + EPS) * g_ref[:, vs]
            og = o_ref[rows, vs].astype(F32)
            zg = z_ref[rows, vs].astype(F32)
            ya_ref[rows, vs] = (hn * _sigmoid(og) * (zg * _sigmoid(zg))).astype(ya_ref.dtype)

            wk = k.astype(F32) * (wkf * scale)
            upd = lax.dot_general(wk.astype(BF16), v, (((0,), (0,)), ((), ())), preferred_element_type=F32)
            c_ref[h] = decay * c_st + upd
            n_ref[h] = decay * n_st + jnp.sum(wk, axis=0, keepdims=True)
        return carry

    lax.fori_loop(0, t_chunks, chunk_body, 0)


def _mlstm(p, colm, erow, c0, n0, g, *, b, s, d, hm):
    dqk = d // 2 // hm
    dv = d // hm
    t_chunks = _pick(s // CHUNK, (4, 2, 1))
    rb = t_chunks * CHUNK
    nt = s // rb
    hw = d // 2
    row = lambda bi, t: bi * nt + t
    return pl.pallas_call(
        functools.partial(_mlstm_kernel, t_chunks=t_chunks, hm=hm, dqk=dqk, dv=dv, scale=float(dqk) ** -0.5),
        grid=(b, nt),
        in_specs=[pl.BlockSpec((rb, hw), lambda bi, t: (row(bi, t), 0)),
                  pl.BlockSpec((rb, hw), lambda bi, t: (row(bi, t), 1)),
                  pl.BlockSpec((rb, d), lambda bi, t: (row(bi, t), 1)),
                  pl.BlockSpec((rb, d), lambda bi, t: (row(bi, t), 2)),
                  pl.BlockSpec((rb, d), lambda bi, t: (row(bi, t), 3)),
                  pl.BlockSpec((1, rb, LANES), lambda bi, t: (bi, t, 0)),
                  pl.BlockSpec((1, t_chunks, hm, CHUNK), lambda bi, t: (bi, t, 0, 0)),
                  pl.BlockSpec((hm, dqk, dv), lambda bi, t: (0, 0, 0)),
                  pl.BlockSpec((hm, 1, dqk), lambda bi, t: (0, 0, 0)),
                  pl.BlockSpec((1, d), lambda bi, t: (0, 0))],
        out_specs=[pl.BlockSpec((rb, d), lambda bi, t: (row(bi, t), 0)),
                   pl.BlockSpec((None, hm, dqk, dv), lambda bi, t: (bi, 0, 0, 0)),
                   pl.BlockSpec((None, hm, 1, dqk), lambda bi, t: (bi, 0, 0, 0))],
        out_shape=[jax.ShapeDtypeStruct((b * s, d), BF16),
                   jax.ShapeDtypeStruct((b, hm, dqk, dv), F32),
                   jax.ShapeDtypeStruct((b, hm, 1, dqk), F32)],
        compiler_params=_params(("arbitrary", "arbitrary"), VMEM_LIMIT),
        name="mlstm",
    )(p, p, p, p, p, colm, erow, c0, n0, g.reshape(1, d).astype(F32))


def _gdn_kernel(q_ref, k_ref, v_ref, z_ref, colg_ref, grow_ref, s0_ref, g_ref, yb_ref, s_ref, qk_scr, u_scr, wq_scr,
                *, t_chunks, hg, dg):
    @pl.when(pl.program_id(2) == 0)
    def _():
        s_ref[...] = s0_ref[...]

    npk = hg // _PACK
    side = _PACK * CHUNK
    row = lax.broadcasted_iota(jnp.int32, (CHUNK, side), 0)
    lane = lax.broadcasted_iota(jnp.int32, (CHUNK, side), 1)
    lj = lane % CHUNK
    lgrp = lane // CHUNK
    incl = lj <= row
    strict = lj < row
    eye = (lj == row).astype(F32)
    bd_mask = (lax.broadcasted_iota(jnp.int32, (side, side), 0) // CHUNK
               == lax.broadcasted_iota(jnp.int32, (side, side), 1) // CHUNK)
    kgrp = lax.broadcasted_iota(jnp.int32, (CHUNK, _PACK * dg), 1) // dg
    tdims = (((0,), (0,)), ((), ()))

    def blockdiag(x):
        return jnp.where(bd_mask, jnp.concatenate([x] * _PACK, axis=0), jnp.zeros((), x.dtype))

    def spread(colg, j, g):
        base = j * hg + g * _PACK
        acc = jnp.broadcast_to(colg[:, base:base + 1], (CHUNK, side))
        for hq in range(1, _PACK):
            acc = jnp.where(lgrp == hq, colg[:, base + hq:base + hq + 1], acc)
        return acc

    units = [(c, g) for c in range(t_chunks) for g in range(npk)]
    us = range(len(units))
    rws = [slice(c * CHUNK, (c + 1) * CHUNK) for c, _ in units]
    lns = [slice(g * _PACK * dg, (g + 1) * _PACK * dg) for _, g in units]
    colgs = [colg_ref[0, rws[u], :] for u in us]
    k4 = [k_ref[rws[u], lns[u]] for u in us]
    q4 = [q_ref[rws[u], lns[u]] for u in us]
    kbd = [jnp.concatenate([jnp.where(kgrp == hq, k4[u], jnp.zeros((), k4[u].dtype)) for hq in range(_PACK)], axis=0)
           for u in us]
    qkkk = [lax.dot_general(jnp.concatenate([q4[u], k4[u]], axis=0), kbd[u], _NT, preferred_element_type=F32)
            for u in us]
    dec = [jnp.exp(jnp.where(incl, spread(colgs[u], 0, units[u][1])
                             - grow_ref[0, units[u][0], 0, 0, units[u][1]:units[u][1] + 1, :], NEG)) for u in us]
    for u in us:
        qk_scr[units[u][0], units[u][1]] = (qkkk[u][:CHUNK] * dec[u]).astype(qk_scr.dtype)
    a = [jnp.where(strict, qkkk[u][CHUNK:] * dec[u], 0.0) * spread(colgs[u], 1, units[u][1]) for u in us]
    tm = [eye - a[u] for u in us]
    ab = [a[u].astype(BF16) for u in us]
    pw = [jnp.dot(ab[u], blockdiag(ab[u]), preferred_element_type=F32) for u in us]
    for _ in range(4):
        pb = [pw[u].astype(BF16) for u in us]
        both = [jnp.dot(jnp.concatenate([tm[u].astype(BF16), pb[u]], axis=0), blockdiag(pb[u]),
                        preferred_element_type=F32) for u in us]
        tm = [tm[u] + both[u][:CHUNK] for u in us]
        pw = [both[u][CHUNK:] for u in us]
    tb = [((tm[u] + jnp.dot(tm[u].astype(BF16), blockdiag(pw[u].astype(BF16)), preferred_element_type=F32))
           * grow_ref[0, units[u][0], 0, 1, units[u][1]:units[u][1] + 1, :]).astype(BF16) for u in us]

    def pad_rows(x, hq):
        parts = []
        if hq:
            parts.append(jnp.zeros((hq * CHUNK, x.shape[1]), x.dtype))
        parts.append(x)
        if hq < _PACK - 1:
            parts.append(jnp.zeros(((_PACK - 1 - hq) * CHUNK, x.shape[1]), x.dtype))
        return jnp.concatenate(parts, axis=0)

    items = [(u, hq) for u in us for hq in range(_PACK)]
    heads = [units[u][1] * _PACK + hq for u, hq in items]
    hsl = [slice(h * dg, (h + 1) * dg) for h in heads]
    expgc = [colgs[u][:, 2 * hg + h:2 * hg + h + 1] for (u, _), h in zip(items, heads)]
    rhs = [jnp.concatenate([v_ref[rws[u], hsl[i]], (k_ref[rws[u], hsl[i]].astype(F32) * expgc[i]).astype(BF16)], axis=1)
           for i, (u, _) in enumerate(items)]
    uw = [jnp.dot(tb[u], pad_rows(rhs[i], hq), preferred_element_type=F32) for i, (u, hq) in enumerate(items)]
    for i, (u, _) in enumerate(items):
        c, h = units[u][0], heads[i]
        u_scr[c, h] = uw[i][:, :dg]
        wq_scr[c, h, 0:CHUNK, :] = uw[i][:, dg:].astype(wq_scr.dtype)
        wq_scr[c, h, CHUNK:2 * CHUNK, :] = (q_ref[rws[u], hsl[i]].astype(F32) * expgc[i]).astype(wq_scr.dtype)

    def chunk_body(c, carry):
        r0 = pl.multiple_of(c * CHUNK, CHUNK)
        rows = pl.ds(r0, CHUNK)
        colg = colg_ref[0, rows, :]
        hs = range(hg)
        sls = [slice(h * dg, (h + 1) * dg) for h in hs]
        col = lambda j, h: colg[:, j * hg + h:j * hg + h + 1]
        s_st = [s_ref[h] for h in hs]
        s_b = [s_st[h].astype(BF16) for h in hs]
        wqs = [jnp.dot(wq_scr[c, h], s_b[h], preferred_element_type=F32) for h in hs]
        v_nb = [(u_scr[c, h] - wqs[h][:CHUNK]).astype(BF16) for h in hs]
        o = [wqs[h][CHUNK:] + jnp.dot(qk_scr[c, h // _PACK], pad_rows(v_nb[h], h % _PACK), preferred_element_type=F32)
             for h in hs]
        k_dec = [(k_ref[rows, sls[h]].astype(F32) * col(3, h)).astype(BF16) for h in hs]
        for h in hs:
            expgl = colg[0:1, 4 * hg + h:4 * hg + h + 1]
            s_ref[h] = expgl * s_st[h] + lax.dot_general(k_dec[h], v_nb[h], tdims, preferred_element_type=F32)
        for h in hs:
            ms = jnp.mean(o[h] * o[h], axis=1, keepdims=True)
            zg = z_ref[rows, sls[h]].astype(F32)
            yb_ref[rows, sls[h]] = (o[h] * lax.rsqrt(ms + EPS) * g_ref[...] * (zg * _sigmoid(zg))).astype(yb_ref.dtype)
        return carry

    lax.fori_loop(0, t_chunks, chunk_body, 0, unroll=2 if t_chunks % 2 == 0 else 1)


def _gdn(p, colg, grow, s0, g, *, b, s, d, hgd, hg, qcol0):
    dg = d // hgd
    groups = hgd // hg
    gw = hg * dg
    npk = hg // _PACK
    side = _PACK * CHUNK
    t_chunks = _pick(s // CHUNK, (8, 4, 2, 1))
    rb = t_chunks * CHUNK
    nt = s // rb
    qb = qcol0 // gw
    row = lambda bi, t: bi * nt + t
    return pl.pallas_call(
        functools.partial(_gdn_kernel, t_chunks=t_chunks, hg=hg, dg=dg),
        grid=(b, groups, nt),
        in_specs=[pl.BlockSpec((rb, gw), lambda bi, gi, t: (row(bi, t), qb + gi)),
                  pl.BlockSpec((rb, gw), lambda bi, gi, t: (row(bi, t), qb + groups + gi)),
                  pl.BlockSpec((rb, gw), lambda bi, gi, t: (row(bi, t), qb + 2 * groups + gi)),
                  pl.BlockSpec((rb, gw), lambda bi, gi, t: (row(bi, t), qb + 3 * groups + gi)),
                  pl.BlockSpec((1, rb, LANES), lambda bi, gi, t: (bi, t, gi)),
                  pl.BlockSpec((1, t_chunks, 1, 2, npk, side), lambda bi, gi, t: (bi, t, gi, 0, 0, 0)),
                  pl.BlockSpec((hg, dg, dg), lambda bi, gi, t: (gi, 0, 0)),
                  pl.BlockSpec((1, dg), lambda bi, gi, t: (0, 0))],
        out_specs=[pl.BlockSpec((rb, gw), lambda bi, gi, t: (row(bi, t), gi)),
                   pl.BlockSpec((None, hg, dg, dg), lambda bi, gi, t: (bi, gi, 0, 0))],
        out_shape=[jax.ShapeDtypeStruct((b * s, d), BF16),
                   jax.ShapeDtypeStruct((b, hgd, dg, dg), F32)],
        scratch_shapes=[pltpu.VMEM((t_chunks, npk, CHUNK, side), BF16),
                        pltpu.VMEM((t_chunks, hg, CHUNK, dg), F32),
                        pltpu.VMEM((t_chunks, hg, 2 * CHUNK, dg), BF16)],
        compiler_params=_params(("arbitrary", "arbitrary", "arbitrary"), VMEM_LIMIT),
        name="gdn",
    )(p, p, p, p, colg, grow, s0, g.reshape(1, dg).astype(F32))


def _merge_kernel(ya_ref, yb_ref, wa_ref, wb_ref, ga_ref, gb_ref, o_ref):
    pa = jnp.dot(ya_ref[...], wa_ref[...], preferred_element_type=F32)
    pb = jnp.dot(yb_ref[...], wb_ref[...], preferred_element_type=F32)
    o_ref[...] = (_sigmoid(ga_ref[...].astype(F32)) * pa + _sigmoid(gb_ref[...].astype(F32)) * pb).astype(o_ref.dtype)


def _merge(ya, yb, wa, wb, p, *, gcol0):
    m, d = ya.shape
    tm = _pick(m, (256, 128, 64))
    tn = _pick(d, (1024, 512, 256, 128))
    ga0 = gcol0 // tn
    gb0 = (gcol0 + d) // tn
    return pl.pallas_call(
        _merge_kernel,
        grid=(d // tn, m // tm),
        in_specs=[pl.BlockSpec((tm, d), lambda j, i: (i, 0)),
                  pl.BlockSpec((tm, d), lambda j, i: (i, 0)),
                  pl.BlockSpec((d, tn), lambda j, i: (0, j)),
                  pl.BlockSpec((d, tn), lambda j, i: (0, j)),
                  pl.BlockSpec((tm, tn), lambda j, i: (i, ga0 + j)),
                  pl.BlockSpec((tm, tn), lambda j, i: (i, gb0 + j))],
        out_specs=pl.BlockSpec((tm, tn), lambda j, i: (i, j)),
        out_shape=jax.ShapeDtypeStruct((m, d), BF16),
        compiler_params=_params(("parallel", "parallel"), VMEM_LIMIT),
        name="merge",
    )(ya, yb, wa, wb, p, p)


def _outproj_kernel(m_ref, w_ref, x_ref, g_ref, o_ref, ss_ref, *, tn, nj, d):
    j = pl.program_id(1)
    y = x_ref[...] + jnp.dot(m_ref[...], w_ref[...], preferred_element_type=F32)
    part = jnp.sum(y * y, axis=1, keepdims=True)

    @pl.when(j == 0)
    def _():
        ss_ref[...] = part

    @pl.when(j > 0)
    def _():
        ss_ref[...] += part

    for jj in range(nj):
        @pl.when(j == jj)
        def _(jj=jj):
            o_ref[:, jj * tn:(jj + 1) * tn] = y

    @pl.when(j == nj - 1)
    def _():
        o_ref[...] = o_ref[...] * lax.rsqrt(ss_ref[...] / d + EPS) * g_ref[...]


def _outproj(merged, wo, x2d, g):
    m, d = merged.shape
    tm = _pick(m, (512, 256, 128, 64))
    tn = _pick(d, (512, 256, 128))
    nj = d // tn
    return pl.pallas_call(
        functools.partial(_outproj_kernel, tn=tn, nj=nj, d=d),
        grid=(m // tm, nj),
        in_specs=[pl.BlockSpec((tm, d), lambda i, j: (i, 0)),
                  pl.BlockSpec((d, tn), lambda i, j: (0, j)),
                  pl.BlockSpec((tm, tn), lambda i, j: (i, j)),
                  pl.BlockSpec((1, d), lambda i, j: (0, 0))],
        out_specs=pl.BlockSpec((tm, d), lambda i, j: (i, 0)),
        out_shape=jax.ShapeDtypeStruct((m, d), F32),
        scratch_shapes=[pltpu.VMEM((tm, 1), F32)],
        compiler_params=_params(("parallel", "arbitrary"), VMEM_LIMIT),
        name="outproj",
    )(merged, wo, x2d, g.reshape(1, d).astype(F32))


def _pad_lanes(a, width=LANES):
    return jnp.pad(a, [(0, 0)] * (a.ndim - 1) + [(0, width - a.shape[-1])])


def kernel(x, meta, norm_in_g, w_in, b_igate, b_fgate, ml_norm_g, conv_w, a_log, dt_bias,
           gdn_norm_g, w_proj_a, w_proj_b, w_out, norm_f_g):
    assert norm_in_g.shape[0] == 1, "single-layer block"
    b, s, d = x.shape
    hm = b_igate.shape[-1]
    hgd = a_log.shape[-1]
    dg = d // hgd
    hg = min(8, hgd)
    groups = hgd // hg
    assert s % CHUNK == 0 and meta.shape[0] == N_META and 2 * hm + 2 * hgd <= LANES
    n_chunks = s // CHUNK
    s_full = s + CHUNK

    wt = jnp.swapaxes(w_in[0], 0, 1)
    g0 = 4 * d
    g1 = g0 + 2 * hm
    g2 = g1 + 4 * d
    g3 = g2 + 2 * hgd
    w_gate = jnp.pad(jnp.concatenate([wt[g0:g1], wt[g2:g3]], axis=0), ((0, LANES - 2 * hm - 2 * hgd), (0, 0)))
    wa = w_proj_a[0].astype(BF16)
    wb = w_proj_b[0].astype(BF16)
    wo = w_out[0].astype(BF16)

    x2d = x.reshape(b * s, d)
    h0 = jnp.concatenate([jnp.zeros((N_PAD, d), x.dtype), meta.astype(x.dtype)], axis=0)
    hn_main = _rmsnorm(x2d, norm_in_g[0], BF16)
    hn_meta = _rmsnorm(h0, norm_in_g[0], BF16)
    p_main, p_meta = _inproj(hn_main, hn_meta, wt, conv_w[0], n_out=10 * d, seg_starts=(4 * d, 8 * d),
                             seg_skip=(2 * hm, 2 * hgd), conv_col0=4 * d, seq_rows=s, dg=dg)
    gs_main = _gateproj(hn_main, w_gate)
    gs_meta = _gateproj(hn_meta, w_gate)

    gs_full = jnp.concatenate([jnp.broadcast_to(gs_meta[None], (b, CHUNK, LANES)),
                               gs_main.reshape(b, s, LANES)], axis=1)
    gin = jnp.stack([_pad_lanes(gs_full[..., 0:hm]), _pad_lanes(gs_full[..., hm:2 * hm]),
                     _pad_lanes(gs_full[..., 2 * hm:2 * hm + hgd]),
                     _pad_lanes(gs_full[..., 2 * hm + hgd:2 * hm + 2 * hgd])], axis=1)
    prm = jnp.pad(jnp.stack([_pad_lanes(v.astype(F32)) for v in (b_igate[0], b_fgate[0], a_log[0], dt_bias[0])]),
                  ((0, 4), (0, 0)))
    planes = _gates(gin, prm)

    colm = _pad_lanes(planes[:, 0:5, :, 0:hm].transpose(0, 2, 1, 3).reshape(b, s_full, 5 * hm))
    colg = planes[:, 5:10, :, 0:hgd].reshape(b, 5, s_full, groups, hg).transpose(0, 2, 3, 1, 4)
    colg = _pad_lanes(colg.reshape(b, s_full, groups, 5 * hg)).reshape(b, s_full, groups * LANES)
    erow = planes[:, _P_E, :, 0:hm].reshape(b, n_chunks + 1, CHUNK, hm).transpose(0, 1, 3, 2)
    grow = planes[:, _P_GC:_P_BETA + 1, :, 0:hgd].reshape(b, 2, n_chunks + 1, CHUNK, groups, hg)
    grow = grow.transpose(0, 2, 4, 1, 5, 3).reshape(b, n_chunks + 1, groups, 2, hg // _PACK, _PACK * CHUNK)

    dqk = d // 2 // hm
    dv = d // hm
    c0 = jnp.zeros((hm, dqk, dv), F32)
    n0 = jnp.zeros((hm, 1, dqk), F32)
    _, c1, n1 = _mlstm(p_meta, colm[:1, :CHUNK], erow[:1, :1], c0, n0, ml_norm_g[0], b=1, s=CHUNK, d=d, hm=hm)
    ya, _, _ = _mlstm(p_main, colm[:, CHUNK:], erow[:, 1:], c1[0], n1[0], ml_norm_g[0], b=b, s=s, d=d, hm=hm)

    s0 = jnp.zeros((hgd, dg, dg), F32)
    _, s1 = _gdn(p_meta, colg[:1, :CHUNK], grow[:1, :1], s0, gdn_norm_g[0],
                 b=1, s=CHUNK, d=d, hgd=hgd, hg=hg, qcol0=4 * d)
    yb, _ = _gdn(p_main, colg[:, CHUNK:], grow[:, 1:], s1[0], gdn_norm_g[0],
                 b=b, s=s, d=d, hgd=hgd, hg=hg, qcol0=4 * d)

    merged = _merge(ya, yb, wa, wb, p_main, gcol0=8 * d)
    return _outproj(merged, wo, x2d, norm_f_g).reshape(b, s, d)
```

```python
import functools

import jax
import jax.numpy as jnp
from jax import lax
from jax.experimental import pallas as pl
from jax.experimental.pallas import tpu as pltpu

CHUNK = 64
N_META = 16
N_PAD = CHUNK - N_META
EPS = 1e-6
NEG = -1e30
GATE_CAP = 15.0
LANES = 128
SUBLANES = 8
VMEM_LIMIT = 56 * 1024 * 1024
_CONV_SPLIT = 2
_MXU_TILE = 256
_PACK = _MXU_TILE // CHUNK

F32 = jnp.float32
BF16 = jnp.bfloat16

_P_C, _P_WINTER, _P_FLOOR, _P_WKF, _P_DECAY = range(5)


def _sigmoid(x):
    return 1.0 / (1.0 + jnp.exp(-x))


def _softplus(x):
    return jnp.maximum(x, 0.0) + jnp.log1p(jnp.exp(-jnp.abs(x)))


def _pick(n, cands):
    for c in cands:
        if c <= n and n % c == 0:
            return c
    return n


def _params(sem, vmem=None):
    return pltpu.CompilerParams(dimension_semantics=sem, vmem_limit_bytes=vmem)


_NT = (((1,), (1,)), ((), ()))


def _norm_gate_kernel(x_ref, g_ref, ws_ref, o_ref, gs_ref):
    x = x_ref[...].astype(F32)
    ms = jnp.mean(x * x, axis=-1, keepdims=True)
    hn = (x * lax.rsqrt(ms + EPS) * g_ref[...]).astype(o_ref.dtype)
    o_ref[...] = hn
    gs_ref[...] = lax.dot_general(hn, ws_ref[...].astype(hn.dtype), _NT, preferred_element_type=F32)


def _norm_gate(x2d, g, ws):
    m, d = x2d.shape
    tr = _pick(m, (256, 128, 64))
    return pl.pallas_call(
        _norm_gate_kernel,
        grid=(m // tr,),
        in_specs=[pl.BlockSpec((tr, d), lambda i: (i, 0)),
                  pl.BlockSpec((1, d), lambda i: (0, 0)),
                  pl.BlockSpec((LANES, d), lambda i: (0, 0))],
        out_specs=[pl.BlockSpec((tr, d), lambda i: (i, 0)),
                   pl.BlockSpec((tr, LANES), lambda i: (i, 0))],
        out_shape=[jax.ShapeDtypeStruct((m, d), BF16),
                   jax.ShapeDtypeStruct((m, LANES), F32)],
        compiler_params=_params(("parallel",)),
        name="norm_gate",
    )(x2d, g.reshape(1, d).astype(F32), ws)


def _inproj_kernel(h_ref, hm_ref, cw_ref, wt_hbm, p_ref, pm_ref, wstage, wbf, tail_scr, mtail_scr, sem,
                   *, tn, nj, seg_tiles, seg_skip, conv_j0, conv_nj, seq_tiles, dg, scale):
    j = pl.program_id(0)
    i = pl.program_id(1)
    hr = tail_scr.shape[0]
    is_conv = jnp.logical_and(j >= conv_j0, j < conv_j0 + 3 * conv_nj)
    is_v = j >= conv_j0 + 2 * conv_nj
    qscale = jnp.where(j < conv_j0 + conv_nj, scale, 1.0).astype(F32)

    def tile_copy(jj):
        off = jj * tn
        for tiles, skip in zip(seg_tiles, seg_skip):
            off = off + jnp.where(jj >= tiles, skip, 0)
        return pltpu.make_async_copy(wt_hbm.at[pl.ds(pl.multiple_of(off, 8), tn), :], wstage, sem)

    def conv_store(o_ref, acc, halo):
        m = acc.shape[0]
        w = cw_ref[...]
        ntap = w.shape[0]
        ext = jnp.concatenate([halo, acc], axis=0)
        y = w[ntap - 1:ntap, :] * acc
        for tap in range(1, ntap):
            y = y + w[ntap - 1 - tap:ntap - tap, :] * ext[hr - tap:hr - tap + m]
        a = y * _sigmoid(y)
        for hd in range(tn // dg):
            seg = a[:, hd * dg:(hd + 1) * dg]
            ss = jnp.sum(seg * seg, axis=1, keepdims=True)
            fac = jnp.where(is_v, 1.0, lax.rsqrt(ss + EPS) * qscale)
            o_ref[:, hd * dg:(hd + 1) * dg] = (seg * fac).astype(o_ref.dtype)

    @pl.when(i == 0)
    def _():
        @pl.when(j == 0)
        def _():
            tile_copy(j).start()

        tile_copy(j).wait()

        def cast_rows(s, carry):
            rows = pl.ds(pl.multiple_of(s * CHUNK, CHUNK), CHUNK)
            wbf[rows, :] = wstage[rows, :].astype(wbf.dtype)
            return carry

        lax.fori_loop(0, tn // CHUNK, cast_rows, 0)

        @pl.when(j + 1 < nj)
        def _():
            tile_copy(j + 1).start()

        accm = lax.dot_general(hm_ref[...], wbf[...], _NT, preferred_element_type=F32)
        mm = accm.shape[0]

        @pl.when(is_conv)
        def _():
            mtail_scr[...] = accm[mm - hr:mm]
            conv_store(pm_ref, accm, jnp.zeros((hr, tn), F32))

        @pl.when(jnp.logical_not(is_conv))
        def _():
            pm_ref[...] = accm.astype(pm_ref.dtype)

    @pl.when(is_conv)
    def _():
        tm = h_ref.shape[0]
        sub = tm // _CONV_SPLIT
        halo = jnp.where(i % seq_tiles == 0, mtail_scr[...], tail_scr[...])
        accs = []
        for part in range(_CONV_SPLIT):
            accs.append(lax.dot_general(h_ref[part * sub:(part + 1) * sub, :], wbf[...], _NT,
                                        preferred_element_type=F32))
            if part:
                conv_store(p_ref.at[(part - 1) * sub:part * sub, :], accs[part - 1], halo)
                halo = accs[part - 1][sub - hr:]
        conv_store(p_ref.at[tm - sub:tm, :], accs[-1], halo)
        tail_scr[...] = accs[-1][sub - hr:]

    @pl.when(jnp.logical_not(is_conv))
    def _():
        p_ref[...] = lax.dot_general(h_ref[...], wbf[...], _NT, preferred_element_type=F32).astype(p_ref.dtype)


def _inproj(hn, hn_meta, wt, conv_w, *, n_out, seg_starts, seg_skip, conv_col0, seq_rows, dg):
    m, d = hn.shape
    mm = hn_meta.shape[0]
    tm = _pick(seq_rows, (1024, 512, 256, 128, 64))
    tn = _pick(n_out, (1024, 512, 256, 128))
    assert all(st % tn == 0 for st in seg_starts) and all(sk % 8 == 0 for sk in seg_skip)
    assert m % tm == 0 and conv_col0 % tn == 0 and d % tn == 0 and tn % dg == 0 and seq_rows % tm == 0
    nj = n_out // tn
    conv_j0, conv_nj = conv_col0 // tn, d // tn
    return pl.pallas_call(
        functools.partial(_inproj_kernel, tn=tn, nj=nj, seg_tiles=tuple(st // tn for st in seg_starts),
                          seg_skip=tuple(seg_skip), conv_j0=conv_j0, conv_nj=conv_nj, seq_tiles=seq_rows // tm,
                          dg=dg, scale=float(dg) ** -0.5),
        grid=(nj, m // tm),
        in_specs=[pl.BlockSpec((tm, d), lambda j, i: (i, 0)),
                  pl.BlockSpec((mm, d), lambda j, i: (0, 0)),
                  pl.BlockSpec((conv_w.shape[0], tn), lambda j, i: (0, jnp.clip(j - conv_j0, 0, 3 * conv_nj - 1))),
                  pl.BlockSpec(memory_space=pl.ANY)],
        out_specs=[pl.BlockSpec((tm, tn), lambda j, i: (i, j)),
                   pl.BlockSpec((mm, tn), lambda j, i: (0, j))],
        out_shape=[jax.ShapeDtypeStruct((m, n_out), BF16),
                   jax.ShapeDtypeStruct((mm, n_out), BF16)],
        scratch_shapes=[pltpu.VMEM((tn, d), F32), pltpu.VMEM((tn, d), BF16),
                        pltpu.VMEM((SUBLANES, tn), F32), pltpu.VMEM((SUBLANES, tn), F32),
                        pltpu.SemaphoreType.DMA(())],
        compiler_params=_params(("arbitrary", "arbitrary"), VMEM_LIMIT),
        name="inproj",
    )(hn, hn_meta, conv_w.astype(F32), wt)


def _cumsum_rows(x, rowi):
    for d in (1, 2, 4, 8, 16, 32):
        x = x + jnp.where(rowi >= d, pltpu.roll(x, d, axis=0), 0.0)
    return x


def _cummax_rows(x, rowi):
    for d in (1, 2, 4, 8, 16, 32):
        x = jnp.maximum(x, jnp.where(rowi >= d, pltpu.roll(x, d, axis=0), NEG))
    return x


def _pack_lanes(vals, src0, width, lane):
    out = None
    for q, v in enumerate(vals):
        shift = (q * width - src0) % LANES
        r = pltpu.roll(v, shift, axis=1) if shift else v
        out = r if out is None else jnp.where(lane // width == q, r, out)
    return out


def _gates_kernel(gin_ref, prm_ref, colm_ref, colg_ref, small_ref, m_ref, *, tc, hm, hgd, hg):
    t = pl.program_id(1)

    @pl.when(t == 0)
    def _():
        m_ref[...] = jnp.zeros_like(m_ref)

    rowi = lax.broadcasted_iota(jnp.int32, (CHUNK, LANES), 0)
    lane = lax.broadcasted_iota(jnp.int32, (CHUNK, LANES), 1)
    b_i = prm_ref[0:1, :]
    b_f = prm_ref[1:2, :]
    neg_a = -jnp.exp(prm_ref[2:3, :])
    dtb = prm_ref[3:4, :]
    last = CHUNK - 1
    gsrc = 2 * hm
    for ci in range(tc):
        rows = slice(ci * CHUNK, (ci + 1) * CHUNK)
        valid = ((t * tc + ci) * CHUNK + rowi) >= N_PAD
        x = gin_ref[0, rows, :]
        m_f = pltpu.roll(x, LANES - hm, axis=1)
        g_b = pltpu.roll(x, LANES - hgd, axis=1)

        i_pre = GATE_CAP * jnp.tanh((x + b_i) / GATE_CAP)
        f_pre = GATE_CAP * jnp.tanh((m_f + b_f) / GATE_CAP)
        i_pre = jnp.where(valid, i_pre, NEG)
        logf = jnp.where(valid, -_softplus(-f_pre), 0.0)
        bcum = _cumsum_rows(logf, rowi)
        e = i_pre - bcum
        cm = _cummax_rows(e, rowi)
        gtot = bcum[last:last + 1, :]
        m_old = m_ref[0:1, :]
        m_new = jnp.maximum(gtot + m_old, gtot + cm[last:last + 1, :])
        c = jnp.maximum(cm, m_old)
        decay = jnp.broadcast_to(jnp.exp(gtot + m_old - m_new), (CHUNK, LANES))
        colm_ref[0, rows, :] = _pack_lanes(
            [c, jnp.exp(m_old - c), jnp.exp(-(bcum + c)), jnp.exp(gtot + e - m_new), decay], 0, hm, lane)
        small_ref[0, 0, rows, :] = e
        m_ref[...] = jnp.broadcast_to(m_new, m_ref.shape)

        g = jnp.where(valid, neg_a * _softplus(x + dtb), 0.0)
        beta = jnp.where(valid, _sigmoid(g_b), 0.0)
        gc = _cumsum_rows(g, rowi)
        gl = gc[last:last + 1, :]
        small_ref[0, 1, rows, :] = gc
        small_ref[0, 2, rows, :] = beta
        gvals = [gc, beta, jnp.exp(gc), jnp.exp(gl - gc), jnp.broadcast_to(jnp.exp(gl), (CHUNK, LANES))]
        for gi in range(hgd // hg):
            colg_ref[0, rows, gi * LANES:(gi + 1) * LANES] = _pack_lanes(gvals, gsrc + gi * hg, hg, lane)


def _gates(gs, prm, *, hm, hgd, hg):
    b, s_full, _ = gs.shape
    n_full = s_full // CHUNK
    groups = hgd // hg
    tc = _pick(n_full, (3, 4, 2, 1))
    r = tc * CHUNK
    return pl.pallas_call(
        functools.partial(_gates_kernel, tc=tc, hm=hm, hgd=hgd, hg=hg),
        grid=(b, n_full // tc),
        in_specs=[pl.BlockSpec((1, r, LANES), lambda i, t: (i, t, 0)),
                  pl.BlockSpec((SUBLANES, LANES), lambda i, t: (0, 0))],
        out_specs=[pl.BlockSpec((1, r, LANES), lambda i, t: (i, t, 0)),
                   pl.BlockSpec((1, r, groups * LANES), lambda i, t: (i, t, 0)),
                   pl.BlockSpec((1, 3, r, LANES), lambda i, t: (i, 0, t, 0))],
        out_shape=[jax.ShapeDtypeStruct((b, s_full, LANES), F32),
                   jax.ShapeDtypeStruct((b, s_full, groups * LANES), F32),
                   jax.ShapeDtypeStruct((b, 3, s_full, LANES), F32)],
        scratch_shapes=[pltpu.VMEM((SUBLANES, LANES), F32)],
        compiler_params=_params(("arbitrary", "arbitrary")),
        name="gates",
    )(gs, prm)


def _mlstm_kernel(q_ref, k_ref, v_ref, o_ref, z_ref, colm_ref, erow_ref, c0_ref, n0_ref, g_ref,
                  ya_ref, c_ref, n_ref, *, t_chunks, hm, dqk, dv, scale):
    @pl.when(pl.program_id(1) == 0)
    def _():
        c_ref[...] = c0_ref[...]
        n_ref[...] = n0_ref[...]

    ri = lax.broadcasted_iota(jnp.int32, (CHUNK, CHUNK), 0)
    ci = lax.broadcasted_iota(jnp.int32, (CHUNK, CHUNK), 1)
    causal = ci <= ri

    def chunk_body(c, carry):
        r0 = pl.multiple_of(c * CHUNK, CHUNK)
        rows = pl.ds(r0, CHUNK)
        colm = colm_ref[0, rows, :]
        for h in range(hm):
            qs = slice(h * dqk, (h + 1) * dqk)
            vs = slice(h * dv, (h + 1) * dv)
            q = q_ref[rows, qs]
            k = k_ref[rows, qs]
            v = v_ref[rows, vs]
            cc = colm[:, _P_C * hm + h:_P_C * hm + h + 1]
            winter = colm[:, _P_WINTER * hm + h:_P_WINTER * hm + h + 1]
            floor = colm[:, _P_FLOOR * hm + h:_P_FLOOR * hm + h + 1]
            wkf = colm[:, _P_WKF * hm + h:_P_WKF * hm + h + 1]
            decay = colm[0:1, _P_DECAY * hm + h:_P_DECAY * hm + h + 1]
            e_row = erow_ref[0, c, h:h + 1, :]

            qk = lax.dot_general(q, k, (((1,), (1,)), ((), ())), preferred_element_type=F32) * scale
            s = qk * jnp.exp(jnp.where(causal, e_row - cc, NEG))
            den_intra = jnp.sum(s, axis=1, keepdims=True)
            c_st = c_ref[h]
            n_st = n_ref[h]
            num = winter * jnp.dot(q, c_st.astype(BF16), preferred_element_type=F32) + jnp.dot(
                s.astype(BF16), v, preferred_element_type=F32)
            qn = jnp.sum(q.astype(F32) * n_st, axis=1, keepdims=True)
            den = winter * qn + den_intra
            hh = num * (1.0 / jnp.maximum(jnp.abs(den), floor))
            ms = jnp.mean(hh * hh, axis=1, keepdims=True)
            hn = hh * lax.rsqrt(ms + EPS) * g_ref[:, vs]
            og = o_ref[rows, vs].astype(F32)
            zg = z_ref[rows, vs].astype(F32)
            ya_ref[rows, vs] = (hn * _sigmoid(og) * (zg * _sigmoid(zg))).astype(ya_ref.dtype)

            wk = k.astype(F32) * (wkf * scale)
            upd = lax.dot_general(wk.astype(BF16), v, (((0,), (0,)), ((), ())), preferred_element_type=F32)
            c_ref[h] = decay * c_st + upd
            n_ref[h] = decay * n_st + jnp.sum(wk, axis=0, keepdims=True)
        return carry

    lax.fori_loop(0, t_chunks, chunk_body, 0)


def _mlstm(p, colm, erow, c0, n0, g, *, b, s, d, hm):
    dqk = d // 2 // hm
    dv = d // hm
    t_chunks = _pick(s // CHUNK, (4, 2, 1))
    rb = t_chunks * CHUNK
    nt = s // rb
    hw = d // 2
    row = lambda bi, t: bi * nt + t
    return pl.pallas_call(
        functools.partial(_mlstm_kernel, t_chunks=t_chunks, hm=hm, dqk=dqk, dv=dv, scale=float(dqk) ** -0.5),
        grid=(b, nt),
        in_specs=[pl.BlockSpec((rb, hw), lambda bi, t: (row(bi, t), 0)),
                  pl.BlockSpec((rb, hw), lambda bi, t: (row(bi, t), 1)),
                  pl.BlockSpec((rb, d), lambda bi, t: (row(bi, t), 1)),
                  pl.BlockSpec((rb, d), lambda bi, t: (row(bi, t), 2)),
                  pl.BlockSpec((rb, d), lambda bi, t: (row(bi, t), 3)),
                  pl.BlockSpec((1, rb, LANES), lambda bi, t: (bi, t, 0)),
                  pl.BlockSpec((1, t_chunks, hm, CHUNK), lambda bi, t: (bi, t, 0, 0)),
                  pl.BlockSpec((hm, dqk, dv), lambda bi, t: (0, 0, 0)),
                  pl.BlockSpec((hm, 1, dqk), lambda bi, t: (0, 0, 0)),
                  pl.BlockSpec((1, d), lambda bi, t: (0, 0))],
        out_specs=[pl.BlockSpec((rb, d), lambda bi, t: (row(bi, t), 0)),
                   pl.BlockSpec((None, hm, dqk, dv), lambda bi, t: (bi, 0, 0, 0)),
                   pl.BlockSpec((None, hm, 1, dqk), lambda bi, t: (bi, 0, 0, 0))],
        out_shape=[jax.ShapeDtypeStruct((b * s, d), BF16),
                   jax.ShapeDtypeStruct((b, hm, dqk, dv), F32),
                   jax.ShapeDtypeStruct((b, hm, 1, dqk), F32)],
        compiler_params=_params(("arbitrary", "arbitrary"), VMEM_LIMIT),
        name="mlstm",
    )(p, p, p, p, p, colm, erow, c0, n0, g.reshape(1, d).astype(F32))


def _gdn_kernel(q_ref, k_ref, v_ref, z_ref, colg_ref, grow_ref, s0_ref, g_ref, yb_ref, s_ref, qk_scr, u_scr, wq_scr,
                *, t_chunks, hg, dg):
    @pl.when(pl.program_id(2) == 0)
    def _():
        s_ref[...] = s0_ref[...]

    npk = hg // _PACK
    side = _PACK * CHUNK
    row = lax.broadcasted_iota(jnp.int32, (CHUNK, side), 0)
    lane = lax.broadcasted_iota(jnp.int32, (CHUNK, side), 1)
    lj = lane % CHUNK
    lgrp = lane // CHUNK
    incl = lj <= row
    strict = lj < row
    eye = (lj == row).astype(F32)
    bd_mask = (lax.broadcasted_iota(jnp.int32, (side, side), 0) // CHUNK
               == lax.broadcasted_iota(jnp.int32, (side, side), 1) // CHUNK)
    kgrp = lax.broadcasted_iota(jnp.int32, (CHUNK, _PACK * dg), 1) // dg
    tdims = (((0,), (0,)), ((), ()))

    def blockdiag(x):
        return jnp.where(bd_mask, jnp.concatenate([x] * _PACK, axis=0), jnp.zeros((), x.dtype))

    def spread(colg, j, g):
        base = j * hg + g * _PACK
        acc = jnp.broadcast_to(colg[:, base:base + 1], (CHUNK, side))
        for hq in range(1, _PACK):
            acc = jnp.where(lgrp == hq, colg[:, base + hq:base + hq + 1], acc)
        return acc

    units = [(c, g) for c in range(t_chunks) for g in range(npk)]
    us = range(len(units))
    rws = [slice(c * CHUNK, (c + 1) * CHUNK) for c, _ in units]
    lns = [slice(g * _PACK * dg, (g + 1) * _PACK * dg) for _, g in units]
    colgs = [colg_ref[0, rws[u], :] for u in us]
    k4 = [k_ref[rws[u], lns[u]] for u in us]
    q4 = [q_ref[rws[u], lns[u]] for u in us]
    kbd = [jnp.concatenate([jnp.where(kgrp == hq, k4[u], jnp.zeros((), k4[u].dtype)) for hq in range(_PACK)], axis=0)
           for u in us]
    qkkk = [lax.dot_general(jnp.concatenate([q4[u], k4[u]], axis=0), kbd[u], _NT, preferred_element_type=F32)
            for u in us]
    dec = [jnp.exp(jnp.where(incl, spread(colgs[u], 0, units[u][1])
                             - grow_ref[0, units[u][0], 0, 0, units[u][1]:units[u][1] + 1, :], NEG)) for u in us]
    for u in us:
        qk_scr[units[u][0], units[u][1]] = (qkkk[u][:CHUNK] * dec[u]).astype(qk_scr.dtype)
    a = [jnp.where(strict, qkkk[u][CHUNK:] * dec[u], 0.0) * spread(colgs[u], 1, units[u][1]) for u in us]
    tm = [eye - a[u] for u in us]
    ab = [a[u].astype(BF16) for u in us]
    pw = [jnp.dot(ab[u], blockdiag(ab[u]), preferred_element_type=F32) for u in us]
    for _ in range(4):
        pb = [pw[u].astype(BF16) for u in us]
        both = [jnp.dot(jnp.concatenate([tm[u].astype(BF16), pb[u]], axis=0), blockdiag(pb[u]),
                        preferred_element_type=F32) for u in us]
        tm = [tm[u] + both[u][:CHUNK] for u in us]
        pw = [both[u][CHUNK:] for u in us]
    tb = [((tm[u] + jnp.dot(tm[u].astype(BF16), blockdiag(pw[u].astype(BF16)), preferred_element_type=F32))
           * grow_ref[0, units[u][0], 0, 1, units[u][1]:units[u][1] + 1, :]).astype(BF16) for u in us]

    def pad_rows(x, hq):
        parts = []
        if hq:
            parts.append(jnp.zeros((hq * CHUNK, x.shape[1]), x.dtype))
        parts.append(x)
        if hq < _PACK - 1:
            parts.append(jnp.zeros(((_PACK - 1 - hq) * CHUNK, x.shape[1]), x.dtype))
        return jnp.concatenate(parts, axis=0)

    items = [(u, hq) for u in us for hq in range(_PACK)]
    heads = [units[u][1] * _PACK + hq for u, hq in items]
    hsl = [slice(h * dg, (h + 1) * dg) for h in heads]
    expgc = [colgs[u][:, 2 * hg + h:2 * hg + h + 1] for (u, _), h in zip(items, heads)]
    rhs = [jnp.concatenate([v_ref[rws[u], hsl[i]], (k_ref[rws[u], hsl[i]].astype(F32) * expgc[i]).astype(BF16)], axis=1)
           for i, (u, _) in enumerate(items)]
    uw = [jnp.dot(tb[u], pad_rows(rhs[i], hq), preferred_element_type=F32) for i, (u, hq) in enumerate(items)]
    for i, (u, _) in enumerate(items):
        c, h = units[u][0], heads[i]
        u_scr[c, h] = uw[i][:, :dg]
        wq_scr[c, h, 0:CHUNK, :] = uw[i][:, dg:].astype(wq_scr.dtype)
        wq_scr[c, h, CHUNK:2 * CHUNK, :] = (q_ref[rws[u], hsl[i]].astype(F32) * expgc[i]).astype(wq_scr.dtype)

    def chunk_body(c, carry):
        r0 = pl.multiple_of(c * CHUNK, CHUNK)
        rows = pl.ds(r0, CHUNK)
        colg = colg_ref[0, rows, :]
        hs = range(hg)
        sls = [slice(h * dg, (h + 1) * dg) for h in hs]
        col = lambda j, h: colg[:, j * hg + h:j * hg + h + 1]
        s_st = [s_ref[h] for h in hs]
        s_b = [s_st[h].astype(BF16) for h in hs]
        wqs = [jnp.dot(wq_scr[c, h], s_b[h], preferred_element_type=F32) for h in hs]
        v_nb = [(u_scr[c, h] - wqs[h][:CHUNK]).astype(BF16) for h in hs]
        o = [wqs[h][CHUNK:] + jnp.dot(qk_scr[c, h // _PACK], pad_rows(v_nb[h], h % _PACK), preferred_element_type=F32)
             for h in hs]
        k_dec = [(k_ref[rows, sls[h]].astype(F32) * col(3, h)).astype(BF16) for h in hs]
        for h in hs:
            expgl = colg[0:1, 4 * hg + h:4 * hg + h + 1]
            s_ref[h] = expgl * s_st[h] + lax.dot_general(k_dec[h], v_nb[h], tdims, preferred_element_type=F32)
        for h in hs:
            ms = jnp.mean(o[h] * o[h], axis=1, keepdims=True)
            zg = z_ref[rows, sls[h]].astype(F32)
            yb_ref[rows, sls[h]] = (o[h] * lax.rsqrt(ms + EPS) * g_ref[...] * (zg * _sigmoid(zg))).astype(yb_ref.dtype)
        return carry

    lax.fori_loop(0, t_chunks, chunk_body, 0, unroll=2 if t_chunks % 2 == 0 else 1)


def _gdn(p, colg, grow, s0, g, *, b, s, d, hgd, hg, qcol0):
    dg = d // hgd
    groups = hgd // hg
    gw = hg * dg
    npk = hg // _PACK
    side = _PACK * CHUNK
    t_chunks = _pick(s // CHUNK, (8, 4, 2, 1))
    rb = t_chunks * CHUNK
    nt = s // rb
    qb = qcol0 // gw
    row = lambda bi, t: bi * nt + t
    return pl.pallas_call(
        functools.partial(_gdn_kernel, t_chunks=t_chunks, hg=hg, dg=dg),
        grid=(b, groups, nt),
        in_specs=[pl.BlockSpec((rb, gw), lambda bi, gi, t: (row(bi, t), qb + gi)),
                  pl.BlockSpec((rb, gw), lambda bi, gi, t: (row(bi, t), qb + groups + gi)),
                  pl.BlockSpec((rb, gw), lambda bi, gi, t: (row(bi, t), qb + 2 * groups + gi)),
                  pl.BlockSpec((rb, gw), lambda bi, gi, t: (row(bi, t), qb + 3 * groups + gi)),
                  pl.BlockSpec((1, rb, LANES), lambda bi, gi, t: (bi, t, gi)),
                  pl.BlockSpec((1, t_chunks, 1, 2, npk, side), lambda bi, gi, t: (bi, t, gi, 0, 0, 0)),
                  pl.BlockSpec((hg, dg, dg), lambda bi, gi, t: (gi, 0, 0)),
                  pl.BlockSpec((1, dg), lambda bi, gi, t: (0, 0))],
        out_specs=[pl.BlockSpec((rb, gw), lambda bi, gi, t: (row(bi, t), gi)),
                   pl.BlockSpec((None, hg, dg, dg), lambda bi, gi, t: (bi, gi, 0, 0))],
        out_shape=[jax.ShapeDtypeStruct((b * s, d), BF16),
                   jax.ShapeDtypeStruct((b, hgd, dg, dg), F32)],
        scratch_shapes=[pltpu.VMEM((t_chunks, npk, CHUNK, side), BF16),
                        pltpu.VMEM((t_chunks, hg, CHUNK, dg), F32),
                        pltpu.VMEM((t_chunks, hg, 2 * CHUNK, dg), BF16)],
        compiler_params=_params(("arbitrary", "arbitrary", "arbitrary"), VMEM_LIMIT),
        name="gdn",
    )(p, p, p, p, colg, grow, s0, g.reshape(1, dg).astype(F32))


def _merge_kernel(ya_ref, yb_ref, wa_ref, wb_ref, ga_ref, gb_ref, o_ref):
    pa = jnp.dot(ya_ref[...], wa_ref[...], preferred_element_type=F32)
    pb = jnp.dot(yb_ref[...], wb_ref[...], preferred_element_type=F32)
    o_ref[...] = (_sigmoid(ga_ref[...].astype(F32)) * pa + _sigmoid(gb_ref[...].astype(F32)) * pb).astype(o_ref.dtype)


def _merge(ya, yb, wa, wb, p, *, gcol0):
    m, d = ya.shape
    tm = _pick(m, (256, 128, 64))
    tn = _pick(d, (1024, 512, 256, 128))
    ga0 = gcol0 // tn
    gb0 = (gcol0 + d) // tn
    return pl.pallas_call(
        _merge_kernel,
        grid=(d // tn, m // tm),
        in_specs=[pl.BlockSpec((tm, d), lambda j, i: (i, 0)),
                  pl.BlockSpec((tm, d), lambda j, i: (i, 0)),
                  pl.BlockSpec((d, tn), lambda j, i: (0, j)),
                  pl.BlockSpec((d, tn), lambda j, i: (0, j)),
                  pl.BlockSpec((tm, tn), lambda j, i: (i, ga0 + j)),
                  pl.BlockSpec((tm, tn), lambda j, i: (i, gb0 + j))],
        out_specs=pl.BlockSpec((tm, tn), lambda j, i: (i, j)),
        out_shape=jax.ShapeDtypeStruct((m, d), BF16),
        compiler_params=_params(("parallel", "parallel"), VMEM_LIMIT),
        name="merge",
    )(ya, yb, wa, wb, p, p)


def _outproj_kernel(m_ref, w_ref, x_ref, g_ref, o_ref, ss_ref, *, tn, nj, d):
    j = pl.program_id(1)
    y = x_ref[...] + jnp.dot(m_ref[...], w_ref[...], preferred_element_type=F32)
    part = jnp.sum(y * y, axis=1, keepdims=True)

    @pl.when(j == 0)
    def _():
        ss_ref[...] = part

    @pl.when(j > 0)
    def _():
        ss_ref[...] += part

    for jj in range(nj):
        @pl.when(j == jj)
        def _(jj=jj):
            o_ref[:, jj * tn:(jj + 1) * tn] = y

    @pl.when(j == nj - 1)
    def _():
        o_ref[...] = o_ref[...] * lax.rsqrt(ss_ref[...] / d + EPS) * g_ref[...]


def _outproj(merged, wo, x2d, g):
    m, d = merged.shape
    tm = _pick(m, (512, 256, 128, 64))
    tn = _pick(d, (512, 256, 128))
    nj = d // tn
    return pl.pallas_call(
        functools.partial(_outproj_kernel, tn=tn, nj=nj, d=d),
        grid=(m // tm, nj),
        in_specs=[pl.BlockSpec((tm, d), lambda i, j: (i, 0)),
                  pl.BlockSpec((d, tn), lambda i, j: (0, j)),
                  pl.BlockSpec((tm, tn), lambda i, j: (i, j)),
                  pl.BlockSpec((1, d), lambda i, j: (0, 0))],
        out_specs=pl.BlockSpec((tm, d), lambda i, j: (i, 0)),
        out_shape=jax.ShapeDtypeStruct((m, d), F32),
        scratch_shapes=[pltpu.VMEM((tm, 1), F32)],
        compiler_params=_params(("parallel", "arbitrary"), VMEM_LIMIT),
        name="outproj",
    )(merged, wo, x2d, g.reshape(1, d).astype(F32))


def _pad_lanes(a, width=LANES):
    return jnp.pad(a, [(0, 0)] * (a.ndim - 1) + [(0, width - a.shape[-1])])


def kernel(x, meta, norm_in_g, w_in, b_igate, b_fgate, ml_norm_g, conv_w, a_log, dt_bias,
           gdn_norm_g, w_proj_a, w_proj_b, w_out, norm_f_g):
    assert norm_in_g.shape[0] == 1, "single-layer block"
    b, s, d = x.shape
    hm = b_igate.shape[-1]
    hgd = a_log.shape[-1]
    dg = d // hgd
    hg = min(8, hgd)
    groups = hgd // hg
    assert s % CHUNK == 0 and meta.shape[0] == N_META and 2 * hm + 2 * hgd <= LANES
    n_chunks = s // CHUNK
    s_full = s + CHUNK

    wt = jnp.swapaxes(w_in[0], 0, 1)
    g0 = 4 * d
    g1 = g0 + 2 * hm
    g2 = g1 + 4 * d
    g3 = g2 + 2 * hgd
    w_gate = jnp.pad(jnp.concatenate([wt[g0:g1], wt[g2:g3]], axis=0), ((0, LANES - 2 * hm - 2 * hgd), (0, 0)))
    wa = w_proj_a[0].astype(BF16)
    wb = w_proj_b[0].astype(BF16)
    wo = w_out[0].astype(BF16)

    x2d = x.reshape(b * s, d)
    h0 = jnp.concatenate([jnp.zeros((N_PAD, d), x.dtype), meta.astype(x.dtype)], axis=0)
    hn_main, gs_main = _norm_gate(x2d, norm_in_g[0], w_gate)
    hn_meta, gs_meta = _norm_gate(h0, norm_in_g[0], w_gate)
    p_main, p_meta = _inproj(hn_main, hn_meta, wt, conv_w[0], n_out=10 * d, seg_starts=(4 * d, 8 * d),
                             seg_skip=(2 * hm, 2 * hgd), conv_col0=4 * d, seq_rows=s, dg=dg)

    gs_full = jnp.concatenate([jnp.broadcast_to(gs_meta[None], (b, CHUNK, LANES)),
                               gs_main.reshape(b, s, LANES)], axis=1)
    lane_g = (2 * hm, LANES - 2 * hm - hgd)
    prm = jnp.pad(jnp.stack([_pad_lanes(b_igate[0].astype(F32)), _pad_lanes(b_fgate[0].astype(F32)),
                             jnp.pad(a_log[0].astype(F32), lane_g), jnp.pad(dt_bias[0].astype(F32), lane_g)]),
                  ((0, SUBLANES - 4), (0, 0)))
    colm, colg, small = _gates(gs_full, prm, hm=hm, hgd=hgd, hg=hg)
    erow = small[:, 0, :, 0:hm].reshape(b, n_chunks + 1, CHUNK, hm).transpose(0, 1, 3, 2)
    grow = small[:, 1:3, :, 2 * hm:2 * hm + hgd].reshape(b, 2, n_chunks + 1, CHUNK, groups, hg)
    grow = grow.transpose(0, 2, 4, 1, 5, 3).reshape(b, n_chunks + 1, groups, 2, hg // _PACK, _PACK * CHUNK)

    dqk = d // 2 // hm
    dv = d // hm
    c0 = jnp.zeros((hm, dqk, dv), F32)
    n0 = jnp.zeros((hm, 1, dqk), F32)
    _, c1, n1 = _mlstm(p_meta, colm[:1, :CHUNK], erow[:1, :1], c0, n0, ml_norm_g[0], b=1, s=CHUNK, d=d, hm=hm)
    ya, _, _ = _mlstm(p_main, colm[:, CHUNK:], erow[:, 1:], c1[0], n1[0], ml_norm_g[0], b=b, s=s, d=d, hm=hm)

    s0 = jnp.zeros((hgd, dg, dg), F32)
    _, s1 = _gdn(p_meta, colg[:1, :CHUNK], grow[:1, :1], s0, gdn_norm_g[0],
                 b=1, s=CHUNK, d=d, hgd=hgd, hg=hg, qcol0=4 * d)
    yb, _ = _gdn(p_main, colg[:, CHUNK:], grow[:, 1:], s1[0], gdn_norm_g[0],
                 b=b, s=s, d=d, hgd=hgd, hg=hg, qcol0=4 * d)

    merged = _merge(ya, yb, wa, wb, p_main, gcol0=8 * d)
    return _outproj(merged, wo, x2d, norm_f_g).reshape(b, s, d)
```

```python
import functools

import jax
import jax.numpy as jnp
from jax import lax
from jax.experimental import pallas as pl
from jax.experimental.pallas import tpu as pltpu

CHUNK = 64
N_META = 16
N_PAD = CHUNK - N_META
EPS = 1e-6
NEG = -1e30
GATE_CAP = 15.0
LANES = 128
SUBLANES = 8
VMEM_LIMIT = 56 * 1024 * 1024
_CONV_SPLIT = 2
_MXU_TILE = 256
_PACK = _MXU_TILE // CHUNK

F32 = jnp.float32
BF16 = jnp.bfloat16

_P_C, _P_WINTER, _P_FLOOR, _P_WKF, _P_DECAY = range(5)


def _sigmoid(x):
    return 1.0 / (1.0 + jnp.exp(-x))


def _softplus(x):
    return jnp.maximum(x, 0.0) + jnp.log1p(jnp.exp(-jnp.abs(x)))


def _pick(n, cands):
    for c in cands:
        if c <= n and n % c == 0:
            return c
    return n


def _params(sem, vmem=None):
    return pltpu.CompilerParams(dimension_semantics=sem, vmem_limit_bytes=vmem)


_NT = (((1,), (1,)), ((), ()))


def _norm_gate_kernel(x_ref, g_ref, ws_ref, o_ref, gs_ref):
    x = x_ref[...].astype(F32)
    ms = jnp.mean(x * x, axis=-1, keepdims=True)
    hn = (x * lax.rsqrt(ms + EPS) * g_ref[...]).astype(o_ref.dtype)
    o_ref[...] = hn
    gs_ref[...] = lax.dot_general(hn, ws_ref[...].astype(hn.dtype), _NT, preferred_element_type=F32)


def _norm_gate(x2d, g, ws):
    m, d = x2d.shape
    tr = _pick(m, (256, 128, 64))
    return pl.pallas_call(
        _norm_gate_kernel,
        grid=(m // tr,),
        in_specs=[pl.BlockSpec((tr, d), lambda i: (i, 0)),
                  pl.BlockSpec((1, d), lambda i: (0, 0)),
                  pl.BlockSpec((LANES, d), lambda i: (0, 0))],
        out_specs=[pl.BlockSpec((tr, d), lambda i: (i, 0)),
                   pl.BlockSpec((tr, LANES), lambda i: (i, 0))],
        out_shape=[jax.ShapeDtypeStruct((m, d), BF16),
                   jax.ShapeDtypeStruct((m, LANES), F32)],
        compiler_params=_params(("parallel",)),
        name="norm_gate",
    )(x2d, g.reshape(1, d).astype(F32), ws)


def _inproj_kernel(h_ref, hm_ref, cw_ref, wt_hbm, p_ref, pm_ref, wstage, wbf, tail_scr, mtail_scr, sem,
                   *, tn, nj, seg_tiles, seg_skip, conv_j0, conv_nj, seq_tiles, dg, scale):
    j = pl.program_id(0)
    i = pl.program_id(1)
    hr = tail_scr.shape[0]
    is_conv = jnp.logical_and(j >= conv_j0, j < conv_j0 + 3 * conv_nj)
    is_v = j >= conv_j0 + 2 * conv_nj
    qscale = jnp.where(j < conv_j0 + conv_nj, scale, 1.0).astype(F32)

    def tile_copy(jj):
        off = jj * tn
        for tiles, skip in zip(seg_tiles, seg_skip):
            off = off + jnp.where(jj >= tiles, skip, 0)
        return pltpu.make_async_copy(wt_hbm.at[pl.ds(pl.multiple_of(off, 8), tn), :], wstage, sem)

    def conv_store(o_ref, acc, halo):
        m = acc.shape[0]
        w = cw_ref[...]
        ntap = w.shape[0]
        ext = jnp.concatenate([halo, acc], axis=0)
        y = w[ntap - 1:ntap, :] * acc
        for tap in range(1, ntap):
            y = y + w[ntap - 1 - tap:ntap - tap, :] * ext[hr - tap:hr - tap + m]
        a = y * _sigmoid(y)
        for hd in range(tn // dg):
            seg = a[:, hd * dg:(hd + 1) * dg]
            ss = jnp.sum(seg * seg, axis=1, keepdims=True)
            fac = jnp.where(is_v, 1.0, lax.rsqrt(ss + EPS) * qscale)
            o_ref[:, hd * dg:(hd + 1) * dg] = (seg * fac).astype(o_ref.dtype)

    @pl.when(i == 0)
    def _():
        @pl.when(j == 0)
        def _():
            tile_copy(j).start()

        tile_copy(j).wait()

        def cast_rows(s, carry):
            rows = pl.ds(pl.multiple_of(s * CHUNK, CHUNK), CHUNK)
            wbf[rows, :] = wstage[rows, :].astype(wbf.dtype)
            return carry

        lax.fori_loop(0, tn // CHUNK, cast_rows, 0)

        @pl.when(j + 1 < nj)
        def _():
            tile_copy(j + 1).start()

        accm = lax.dot_general(hm_ref[...], wbf[...], _NT, preferred_element_type=F32)
        mm = accm.shape[0]

        @pl.when(is_conv)
        def _():
            mtail_scr[...] = accm[mm - hr:mm]
            conv_store(pm_ref, accm, jnp.zeros((hr, tn), F32))

        @pl.when(jnp.logical_not(is_conv))
        def _():
            pm_ref[...] = accm.astype(pm_ref.dtype)

    @pl.when(is_conv)
    def _():
        tm = h_ref.shape[0]
        sub = tm // _CONV_SPLIT
        halo = jnp.where(i % seq_tiles == 0, mtail_scr[...], tail_scr[...])
        accs = []
        for part in range(_CONV_SPLIT):
            accs.append(lax.dot_general(h_ref[part * sub:(part + 1) * sub, :], wbf[...], _NT,
                                        preferred_element_type=F32))
            if part:
                conv_store(p_ref.at[(part - 1) * sub:part * sub, :], accs[part - 1], halo)
                halo = accs[part - 1][sub - hr:]
        conv_store(p_ref.at[tm - sub:tm, :], accs[-1], halo)
        tail_scr[...] = accs[-1][sub - hr:]

    @pl.when(jnp.logical_not(is_conv))
    def _():
        p_ref[...] = lax.dot_general(h_ref[...], wbf[...], _NT, preferred_element_type=F32).astype(p_ref.dtype)


def _inproj(hn, hn_meta, wt, conv_w, *, n_out, seg_starts, seg_skip, conv_col0, seq_rows, dg):
    m, d = hn.shape
    mm = hn_meta.shape[0]
    tm = _pick(seq_rows, (1024, 512, 256, 128, 64))
    tn = _pick(n_out, (1024, 512, 256, 128))
    assert all(st % tn == 0 for st in seg_starts) and all(sk % 8 == 0 for sk in seg_skip)
    assert m % tm == 0 and conv_col0 % tn == 0 and d % tn == 0 and tn % dg == 0 and seq_rows % tm == 0
    nj = n_out // tn
    conv_j0, conv_nj = conv_col0 // tn, d // tn
    return pl.pallas_call(
        functools.partial(_inproj_kernel, tn=tn, nj=nj, seg_tiles=tuple(st // tn for st in seg_starts),
                          seg_skip=tuple(seg_skip), conv_j0=conv_j0, conv_nj=conv_nj, seq_tiles=seq_rows // tm,
                          dg=dg, scale=float(dg) ** -0.5),
        grid=(nj, m // tm),
        in_specs=[pl.BlockSpec((tm, d), lambda j, i: (i, 0)),
                  pl.BlockSpec((mm, d), lambda j, i: (0, 0)),
                  pl.BlockSpec((conv_w.shape[0], tn), lambda j, i: (0, jnp.clip(j - conv_j0, 0, 3 * conv_nj - 1))),
                  pl.BlockSpec(memory_space=pl.ANY)],
        out_specs=[pl.BlockSpec((tm, tn), lambda j, i: (i, j)),
                   pl.BlockSpec((mm, tn), lambda j, i: (0, j))],
        out_shape=[jax.ShapeDtypeStruct((m, n_out), BF16),
                   jax.ShapeDtypeStruct((mm, n_out), BF16)],
        scratch_shapes=[pltpu.VMEM((tn, d), F32), pltpu.VMEM((tn, d), BF16),
                        pltpu.VMEM((SUBLANES, tn), F32), pltpu.VMEM((SUBLANES, tn), F32),
                        pltpu.SemaphoreType.DMA(())],
        compiler_params=_params(("arbitrary", "arbitrary"), VMEM_LIMIT),
        name="inproj",
    )(hn, hn_meta, conv_w.astype(F32), wt)


def _cumsum_rows(x, rowi):
    for d in (1, 2, 4, 8, 16, 32):
        x = x + jnp.where(rowi >= d, pltpu.roll(x, d, axis=0), 0.0)
    return x


def _cummax_rows(x, rowi):
    for d in (1, 2, 4, 8, 16, 32):
        x = jnp.maximum(x, jnp.where(rowi >= d, pltpu.roll(x, d, axis=0), NEG))
    return x


def _pack_lanes(vals, src0, width, lane):
    out = None
    for q, v in enumerate(vals):
        shift = (q * width - src0) % LANES
        r = pltpu.roll(v, shift, axis=1) if shift else v
        out = r if out is None else jnp.where(lane // width == q, r, out)
    return out


def _gates_kernel(gin_ref, prm_ref, colm_ref, colg_ref, small_ref, m_ref, *, tc, hm, hgd, hg):
    t = pl.program_id(1)

    @pl.when(t == 0)
    def _():
        m_ref[...] = jnp.zeros_like(m_ref)

    rowi = lax.broadcasted_iota(jnp.int32, (CHUNK, LANES), 0)
    lane = lax.broadcasted_iota(jnp.int32, (CHUNK, LANES), 1)
    b_i = prm_ref[0:1, :]
    b_f = prm_ref[1:2, :]
    neg_a = -jnp.exp(prm_ref[2:3, :])
    dtb = prm_ref[3:4, :]
    last = CHUNK - 1
    gsrc = 2 * hm
    for ci in range(tc):
        rows = slice(ci * CHUNK, (ci + 1) * CHUNK)
        valid = ((t * tc + ci) * CHUNK + rowi) >= N_PAD
        x = gin_ref[0, rows, :]
        m_f = pltpu.roll(x, LANES - hm, axis=1)
        g_b = pltpu.roll(x, LANES - hgd, axis=1)

        i_pre = GATE_CAP * jnp.tanh((x + b_i) / GATE_CAP)
        f_pre = GATE_CAP * jnp.tanh((m_f + b_f) / GATE_CAP)
        i_pre = jnp.where(valid, i_pre, NEG)
        logf = jnp.where(valid, -_softplus(-f_pre), 0.0)
        bcum = _cumsum_rows(logf, rowi)
        e = i_pre - bcum
        cm = _cummax_rows(e, rowi)
        gtot = bcum[last:last + 1, :]
        m_old = m_ref[0:1, :]
        m_new = jnp.maximum(gtot + m_old, gtot + cm[last:last + 1, :])
        c = jnp.maximum(cm, m_old)
        decay = jnp.broadcast_to(jnp.exp(gtot + m_old - m_new), (CHUNK, LANES))
        colm_ref[0, rows, :] = _pack_lanes(
            [c, jnp.exp(m_old - c), jnp.exp(-(bcum + c)), jnp.exp(gtot + e - m_new), decay], 0, hm, lane)
        small_ref[0, 0, rows, :] = e
        m_ref[...] = jnp.broadcast_to(m_new, m_ref.shape)

        g = jnp.where(valid, neg_a * _softplus(x + dtb), 0.0)
        beta = jnp.where(valid, _sigmoid(g_b), 0.0)
        gc = _cumsum_rows(g, rowi)
        gl = gc[last:last + 1, :]
        small_ref[0, 1, rows, :] = gc
        small_ref[0, 2, rows, :] = beta
        gvals = [gc, beta, jnp.exp(gc), jnp.exp(gl - gc), jnp.broadcast_to(jnp.exp(gl), (CHUNK, LANES))]
        for gi in range(hgd // hg):
            colg_ref[0, rows, gi * LANES:(gi + 1) * LANES] = _pack_lanes(gvals, gsrc + gi * hg, hg, lane)


def _gates(gs, prm, *, hm, hgd, hg):
    b, s_full, _ = gs.shape
    n_full = s_full // CHUNK
    groups = hgd // hg
    tc = _pick(n_full, (3, 4, 2, 1))
    r = tc * CHUNK
    return pl.pallas_call(
        functools.partial(_gates_kernel, tc=tc, hm=hm, hgd=hgd, hg=hg),
        grid=(b, n_full // tc),
        in_specs=[pl.BlockSpec((1, r, LANES), lambda i, t: (i, t, 0)),
                  pl.BlockSpec((SUBLANES, LANES), lambda i, t: (0, 0))],
        out_specs=[pl.BlockSpec((1, r, LANES), lambda i, t: (i, t, 0)),
                   pl.BlockSpec((1, r, groups * LANES), lambda i, t: (i, t, 0)),
                   pl.BlockSpec((1, 3, r, LANES), lambda i, t: (i, 0, t, 0))],
        out_shape=[jax.ShapeDtypeStruct((b, s_full, LANES), F32),
                   jax.ShapeDtypeStruct((b, s_full, groups * LANES), F32),
                   jax.ShapeDtypeStruct((b, 3, s_full, LANES), F32)],
        scratch_shapes=[pltpu.VMEM((SUBLANES, LANES), F32)],
        compiler_params=_params(("arbitrary", "arbitrary")),
        name="gates",
    )(gs, prm)


def _mlstm_kernel(q_ref, k_ref, v_ref, o_ref, z_ref, colm_ref, erow_ref, c0_ref, n0_ref, g_ref,
                  ya_ref, c_ref, n_ref, *, t_chunks, hm, dqk, dv, scale):
    @pl.when(pl.program_id(1) == 0)
    def _():
        c_ref[...] = c0_ref[...]
        n_ref[...] = n0_ref[...]

    ri = lax.broadcasted_iota(jnp.int32, (CHUNK, CHUNK), 0)
    ci = lax.broadcasted_iota(jnp.int32, (CHUNK, CHUNK), 1)
    causal = ci <= ri

    def chunk_body(c, carry):
        r0 = pl.multiple_of(c * CHUNK, CHUNK)
        rows = pl.ds(r0, CHUNK)
        colm = colm_ref[0, rows, :]
        for h in range(hm):
            qs = slice(h * dqk, (h + 1) * dqk)
            vs = slice(h * dv, (h + 1) * dv)
            q = q_ref[rows, qs]
            k = k_ref[rows, qs]
            v = v_ref[rows, vs]
            cc = colm[:, _P_C * hm + h:_P_C * hm + h + 1]
            winter = colm[:, _P_WINTER * hm + h:_P_WINTER * hm + h + 1]
            floor = colm[:, _P_FLOOR * hm + h:_P_FLOOR * hm + h + 1]
            wkf = colm[:, _P_WKF * hm + h:_P_WKF * hm + h + 1]
            decay = colm[0:1, _P_DECAY * hm + h:_P_DECAY * hm + h + 1]
            e_row = erow_ref[0, c, h:h + 1, :]

            qk = lax.dot_general(q, k, (((1,), (1,)), ((), ())), preferred_element_type=F32) * scale
            s = qk * jnp.exp(jnp.where(causal, e_row - cc, NEG))
            den_intra = jnp.sum(s, axis=1, keepdims=True)
            c_st = c_ref[h]
            n_st = n_ref[h]
            num = winter * jnp.dot(q, c_st.astype(BF16), preferred_element_type=F32) + jnp.dot(
                s.astype(BF16), v, preferred_element_type=F32)
            qn = jnp.sum(q.astype(F32) * n_st, axis=1, keepdims=True)
            den = winter * qn + den_intra
            hh = num * (1.0 / jnp.maximum(jnp.abs(den), floor))
            ms = jnp.mean(hh * hh, axis=1, keepdims=True)
            hn = hh * lax.rsqrt(ms + EPS) * g_ref[:, vs]
            og = o_ref[rows, vs].astype(F32)
            zg = z_ref[rows, vs].astype(F32)
            ya_ref[rows, vs] = (hn * _sigmoid(og) * (zg * _sigmoid(zg))).astype(ya_ref.dtype)

            wk = k.astype(F32) * (wkf * scale)
            upd = lax.dot_general(wk.astype(BF16), v, (((0,), (0,)), ((), ())), preferred_element_type=F32)
            c_ref[h] = decay * c_st + upd
            n_ref[h] = decay * n_st + jnp.sum(wk, axis=0, keepdims=True)
        return carry

    lax.fori_loop(0, t_chunks, chunk_body, 0)


def _mlstm(p, colm, erow, c0, n0, g, *, b, s, d, hm):
    dqk = d // 2 // hm
    dv = d // hm
    t_chunks = _pick(s // CHUNK, (4, 2, 1))
    rb = t_chunks * CHUNK
    nt = s // rb
    hw = d // 2
    row = lambda bi, t: bi * nt + t
    return pl.pallas_call(
        functools.partial(_mlstm_kernel, t_chunks=t_chunks, hm=hm, dqk=dqk, dv=dv, scale=float(dqk) ** -0.5),
        grid=(b, nt),
        in_specs=[pl.BlockSpec((rb, hw), lambda bi, t: (row(bi, t), 0)),
                  pl.BlockSpec((rb, hw), lambda bi, t: (row(bi, t), 1)),
                  pl.BlockSpec((rb, d), lambda bi, t: (row(bi, t), 1)),
                  pl.BlockSpec((rb, d), lambda bi, t: (row(bi, t), 2)),
                  pl.BlockSpec((rb, d), lambda bi, t: (row(bi, t), 3)),
                  pl.BlockSpec((1, rb, LANES), lambda bi, t: (bi, t, 0)),
                  pl.BlockSpec((1, t_chunks, hm, CHUNK), lambda bi, t: (bi, t, 0, 0)),
                  pl.BlockSpec((hm, dqk, dv), lambda bi, t: (0, 0, 0)),
                  pl.BlockSpec((hm, 1, dqk), lambda bi, t: (0, 0, 0)),
                  pl.BlockSpec((1, d), lambda bi, t: (0, 0))],
        out_specs=[pl.BlockSpec((rb, d), lambda bi, t: (row(bi, t), 0)),
                   pl.BlockSpec((None, hm, dqk, dv), lambda bi, t: (bi, 0, 0, 0)),
                   pl.BlockSpec((None, hm, 1, dqk), lambda bi, t: (bi, 0, 0, 0))],
        out_shape=[jax.ShapeDtypeStruct((b * s, d), BF16),
                   jax.ShapeDtypeStruct((b, hm, dqk, dv), F32),
                   jax.ShapeDtypeStruct((b, hm, 1, dqk), F32)],
        compiler_params=_params(("arbitrary", "arbitrary"), VMEM_LIMIT),
        name="mlstm",
    )(p, p, p, p, p, colm, erow, c0, n0, g.reshape(1, d).astype(F32))


def _gdn_kernel(q_ref, k_ref, v_ref, z_ref, colg_ref, grow_ref, s0_ref, g_ref, yb_ref, s_ref, qk_scr, u_scr, wq_scr,
                *, t_chunks, hg, dg):
    @pl.when(pl.program_id(2) == 0)
    def _():
        s_ref[...] = s0_ref[...]

    npk = hg // _PACK
    side = _PACK * CHUNK
    row = lax.broadcasted_iota(jnp.int32, (CHUNK, side), 0)
    lane = lax.broadcasted_iota(jnp.int32, (CHUNK, side), 1)
    lj = lane % CHUNK
    lgrp = lane // CHUNK
    incl = lj <= row
    strict = lj < row
    eye = (lj == row).astype(F32)
    bd_mask = (lax.broadcasted_iota(jnp.int32, (side, side), 0) // CHUNK
               == lax.broadcasted_iota(jnp.int32, (side, side), 1) // CHUNK)
    kgrp = lax.broadcasted_iota(jnp.int32, (CHUNK, _PACK * dg), 1) // dg
    tdims = (((0,), (0,)), ((), ()))

    def blockdiag(x):
        return jnp.where(bd_mask, jnp.concatenate([x] * _PACK, axis=0), jnp.zeros((), x.dtype))

    def spread(colg, j, g):
        base = j * hg + g * _PACK
        acc = jnp.broadcast_to(colg[:, base:base + 1], (CHUNK, side))
        for hq in range(1, _PACK):
            acc = jnp.where(lgrp == hq, colg[:, base + hq:base + hq + 1], acc)
        return acc

    units = [(c, g) for c in range(t_chunks) for g in range(npk)]
    us = range(len(units))
    rws = [slice(c * CHUNK, (c + 1) * CHUNK) for c, _ in units]
    lns = [slice(g * _PACK * dg, (g + 1) * _PACK * dg) for _, g in units]
    colgs = [colg_ref[0, rws[u], :] for u in us]
    k4 = [k_ref[rws[u], lns[u]] for u in us]
    q4 = [q_ref[rws[u], lns[u]] for u in us]
    kbd = [jnp.concatenate([jnp.where(kgrp == hq, k4[u], jnp.zeros((), k4[u].dtype)) for hq in range(_PACK)], axis=0)
           for u in us]
    qkkk = [lax.dot_general(jnp.concatenate([q4[u], k4[u]], axis=0), kbd[u], _NT, preferred_element_type=F32)
            for u in us]
    dec = [jnp.exp(jnp.where(incl, spread(colgs[u], 0, units[u][1])
                             - grow_ref[0, units[u][0], 0, 0, units[u][1]:units[u][1] + 1, :], NEG)) for u in us]
    for u in us:
        qk_scr[units[u][0], units[u][1]] = (qkkk[u][:CHUNK] * dec[u]).astype(qk_scr.dtype)
    a = [jnp.where(strict, qkkk[u][CHUNK:] * dec[u], 0.0) * spread(colgs[u], 1, units[u][1]) for u in us]
    tm = [eye - a[u] for u in us]
    ab = [a[u].astype(BF16) for u in us]
    pw = [jnp.dot(ab[u], blockdiag(ab[u]), preferred_element_type=F32) for u in us]
    for _ in range(4):
        pb = [pw[u].astype(BF16) for u in us]
        both = [jnp.dot(jnp.concatenate([tm[u].astype(BF16), pb[u]], axis=0), blockdiag(pb[u]),
                        preferred_element_type=F32) for u in us]
        tm = [tm[u] + both[u][:CHUNK] for u in us]
        pw = [both[u][CHUNK:] for u in us]
    tb = [((tm[u] + jnp.dot(tm[u].astype(BF16), blockdiag(pw[u].astype(BF16)), preferred_element_type=F32))
           * grow_ref[0, units[u][0], 0, 1, units[u][1]:units[u][1] + 1, :]).astype(BF16) for u in us]

    def pad_rows(x, hq):
        parts = []
        if hq:
            parts.append(jnp.zeros((hq * CHUNK, x.shape[1]), x.dtype))
        parts.append(x)
        if hq < _PACK - 1:
            parts.append(jnp.zeros(((_PACK - 1 - hq) * CHUNK, x.shape[1]), x.dtype))
        return jnp.concatenate(parts, axis=0)

    items = [(u, hq) for u in us for hq in range(_PACK)]
    heads = [units[u][1] * _PACK + hq for u, hq in items]
    hsl = [slice(h * dg, (h + 1) * dg) for h in heads]
    expgc = [colgs[u][:, 2 * hg + h:2 * hg + h + 1] for (u, _), h in zip(items, heads)]
    rhs = [jnp.concatenate([v_ref[rws[u], hsl[i]], (k_ref[rws[u], hsl[i]].astype(F32) * expgc[i]).astype(BF16)], axis=1)
           for i, (u, _) in enumerate(items)]
    uw = [jnp.dot(tb[u], pad_rows(rhs[i], hq), preferred_element_type=F32) for i, (u, hq) in enumerate(items)]
    for i, (u, _) in enumerate(items):
        c, h = units[u][0], heads[i]
        u_scr[c, h] = uw[i][:, :dg]
        wq_scr[c, h, 0:CHUNK, :] = uw[i][:, dg:].astype(wq_scr.dtype)
        wq_scr[c, h, CHUNK:2 * CHUNK, :] = (q_ref[rws[u], hsl[i]].astype(F32) * expgc[i]).astype(wq_scr.dtype)

    def chunk_body(c, carry):
        r0 = pl.multiple_of(c * CHUNK, CHUNK)
        rows = pl.ds(r0, CHUNK)
        colg = colg_ref[0, rows, :]
        hs = range(hg)
        sls = [slice(h * dg, (h + 1) * dg) for h in hs]
        col = lambda j, h: colg[:, j * hg + h:j * hg + h + 1]
        s_st = [s_ref[h] for h in hs]
        s_b = [s_st[h].astype(BF16) for h in hs]
        wqs = [jnp.dot(wq_scr[c, h], s_b[h], preferred_element_type=F32) for h in hs]
        v_nb = [(u_scr[c, h] - wqs[h][:CHUNK]).astype(BF16) for h in hs]
        o = [wqs[h][CHUNK:] + jnp.dot(qk_scr[c, h // _PACK], pad_rows(v_nb[h], h % _PACK), preferred_element_type=F32)
             for h in hs]
        k_dec = [(k_ref[rows, sls[h]].astype(F32) * col(3, h)).astype(BF16) for h in hs]
        for h in hs:
            expgl = colg[0:1, 4 * hg + h:4 * hg + h + 1]
            s_ref[h] = expgl * s_st[h] + lax.dot_general(k_dec[h], v_nb[h], tdims, preferred_element_type=F32)
        for h in hs:
            ms = jnp.mean(o[h] * o[h], axis=1, keepdims=True)
            zg = z_ref[rows, sls[h]].astype(F32)
            yb_ref[rows, sls[h]] = (o[h] * lax.rsqrt(ms + EPS) * g_ref[...] * (zg * _sigmoid(zg))).astype(yb_ref.dtype)
        return carry

    lax.fori_loop(0, t_chunks, chunk_body, 0, unroll=2 if t_chunks % 2 == 0 else 1)


def _gdn(p, colg, grow, s0, g, *, b, s, d, hgd, hg, qcol0):
    dg = d // hgd
    groups = hgd // hg
    gw = hg * dg
    npk = hg // _PACK
    side = _PACK * CHUNK
    t_chunks = _pick(s // CHUNK, (8, 4, 2, 1))
    rb = t_chunks * CHUNK
    nt = s // rb
    qb = qcol0 // gw
    row = lambda bi, t: bi * nt + t
    return pl.pallas_call(
        functools.partial(_gdn_kernel, t_chunks=t_chunks, hg=hg, dg=dg),
        grid=(b, groups, nt),
        in_specs=[pl.BlockSpec((rb, gw), lambda bi, gi, t: (row(bi, t), qb + gi)),
                  pl.BlockSpec((rb, gw), lambda bi, gi, t: (row(bi, t), qb + groups + gi)),
                  pl.BlockSpec((rb, gw), lambda bi, gi, t: (row(bi, t), qb + 2 * groups + gi)),
                  pl.BlockSpec((rb, gw), lambda bi, gi, t: (row(bi, t), qb + 3 * groups + gi)),
                  pl.BlockSpec((1, rb, LANES), lambda bi, gi, t: (bi, t, gi)),
                  pl.BlockSpec((1, t_chunks, 1, 2, npk, side), lambda bi, gi, t: (bi, t, gi, 0, 0, 0)),
                  pl.BlockSpec((hg, dg, dg), lambda bi, gi, t: (gi, 0, 0)),
                  pl.BlockSpec((1, dg), lambda bi, gi, t: (0, 0))],
        out_specs=[pl.BlockSpec((rb, gw), lambda bi, gi, t: (row(bi, t), gi)),
                   pl.BlockSpec((None, hg, dg, dg), lambda bi, gi, t: (bi, gi, 0, 0))],
        out_shape=[jax.ShapeDtypeStruct((b * s, d), BF16),
                   jax.ShapeDtypeStruct((b, hgd, dg, dg), F32)],
        scratch_shapes=[pltpu.VMEM((t_chunks, npk, CHUNK, side), BF16),
                        pltpu.VMEM((t_chunks, hg, CHUNK, dg), F32),
                        pltpu.VMEM((t_chunks, hg, 2 * CHUNK, dg), BF16)],
        compiler_params=_params(("arbitrary", "arbitrary", "arbitrary"), VMEM_LIMIT),
        name="gdn",
    )(p, p, p, p, colg, grow, s0, g.reshape(1, dg).astype(F32))


def _merge_kernel(ya_ref, yb_ref, wa_ref, wb_ref, ga_ref, gb_ref, o_ref):
    pa = jnp.dot(ya_ref[...], wa_ref[...], preferred_element_type=F32)
    pb = jnp.dot(yb_ref[...], wb_ref[...], preferred_element_type=F32)
    o_ref[...] = (_sigmoid(ga_ref[...].astype(F32)) * pa + _sigmoid(gb_ref[...].astype(F32)) * pb).astype(o_ref.dtype)


def _merge(ya, yb, wa, wb, p, *, gcol0):
    m, d = ya.shape
    tm = _pick(m, (256, 128, 64))
    tn = _pick(d, (1024, 512, 256, 128))
    ga0 = gcol0 // tn
    gb0 = (gcol0 + d) // tn
    return pl.pallas_call(
        _merge_kernel,
        grid=(d // tn, m // tm),
        in_specs=[pl.BlockSpec((tm, d), lambda j, i: (i, 0)),
                  pl.BlockSpec((tm, d), lambda j, i: (i, 0)),
                  pl.BlockSpec((d, tn), lambda j, i: (0, j)),
                  pl.BlockSpec((d, tn), lambda j, i: (0, j)),
                  pl.BlockSpec((tm, tn), lambda j, i: (i, ga0 + j)),
                  pl.BlockSpec((tm, tn), lambda j, i: (i, gb0 + j))],
        out_specs=pl.BlockSpec((tm, tn), lambda j, i: (i, j)),
        out_shape=jax.ShapeDtypeStruct((m, d), BF16),
        compiler_params=_params(("parallel", "parallel"), VMEM_LIMIT),
        name="merge",
    )(ya, yb, wa, wb, p, p)


def _outproj_kernel(m_ref, w_ref, x_ref, g_ref, o_ref, ss_ref, *, tn, nj, d):
    j = pl.program_id(1)
    y = x_ref[...] + jnp.dot(m_ref[...], w_ref[...], preferred_element_type=F32)
    part = jnp.sum(y * y, axis=1, keepdims=True)

    @pl.when(j == 0)
    def _():
        ss_ref[...] = part

    @pl.when(j > 0)
    def _():
        ss_ref[...] += part

    for jj in range(nj):
        @pl.when(j == jj)
        def _(jj=jj):
            o_ref[:, jj * tn:(jj + 1) * tn] = y

    @pl.when(j == nj - 1)
    def _():
        def norm_rows(s, carry):
            rows = pl.ds(pl.multiple_of(s * CHUNK, CHUNK), CHUNK)
            o_ref[rows, :] = o_ref[rows, :] * lax.rsqrt(ss_ref[rows, :] / d + EPS) * g_ref[...]
            return carry

        lax.fori_loop(0, o_ref.shape[0] // CHUNK, norm_rows, 0)


def _outproj(merged, wo, x2d, g):
    m, d = merged.shape
    tm = _pick(m, (1024, 512, 256, 128, 64))
    tn = _pick(d, (256, 128))
    nj = d // tn
    return pl.pallas_call(
        functools.partial(_outproj_kernel, tn=tn, nj=nj, d=d),
        grid=(m // tm, nj),
        in_specs=[pl.BlockSpec((tm, d), lambda i, j: (i, 0), pipeline_mode=pl.Buffered(1)),
                  pl.BlockSpec((d, tn), lambda i, j: (0, j)),
                  pl.BlockSpec((tm, tn), lambda i, j: (i, j)),
                  pl.BlockSpec((1, d), lambda i, j: (0, 0))],
        out_specs=pl.BlockSpec((tm, d), lambda i, j: (i, 0)),
        out_shape=jax.ShapeDtypeStruct((m, d), F32),
        scratch_shapes=[pltpu.VMEM((tm, 1), F32)],
        compiler_params=_params(("parallel", "arbitrary"), VMEM_LIMIT),
        name="outproj",
    )(merged, wo, x2d, g.reshape(1, d).astype(F32))


def _pad_lanes(a, width=LANES):
    return jnp.pad(a, [(0, 0)] * (a.ndim - 1) + [(0, width - a.shape[-1])])


def kernel(x, meta, norm_in_g, w_in, b_igate, b_fgate, ml_norm_g, conv_w, a_log, dt_bias,
           gdn_norm_g, w_proj_a, w_proj_b, w_out, norm_f_g):
    assert norm_in_g.shape[0] == 1, "single-layer block"
    b, s, d = x.shape
    hm = b_igate.shape[-1]
    hgd = a_log.shape[-1]
    dg = d // hgd
    hg = min(8, hgd)
    groups = hgd // hg
    assert s % CHUNK == 0 and meta.shape[0] == N_META and 2 * hm + 2 * hgd <= LANES
    n_chunks = s // CHUNK
    s_full = s + CHUNK

    wt = jnp.swapaxes(w_in[0], 0, 1)
    g0 = 4 * d
    g1 = g0 + 2 * hm
    g2 = g1 + 4 * d
    g3 = g2 + 2 * hgd
    w_gate = jnp.pad(jnp.concatenate([wt[g0:g1], wt[g2:g3]], axis=0), ((0, LANES - 2 * hm - 2 * hgd), (0, 0)))
    wa = w_proj_a[0].astype(BF16)
    wb = w_proj_b[0].astype(BF16)
    wo = w_out[0].astype(BF16)

    x2d = x.reshape(b * s, d)
    h0 = jnp.concatenate([jnp.zeros((N_PAD, d), x.dtype), meta.astype(x.dtype)], axis=0)
    hn_main, gs_main = _norm_gate(x2d, norm_in_g[0], w_gate)
    hn_meta, gs_meta = _norm_gate(h0, norm_in_g[0], w_gate)
    p_main, p_meta = _inproj(hn_main, hn_meta, wt, conv_w[0], n_out=10 * d, seg_starts=(4 * d, 8 * d),
                             seg_skip=(2 * hm, 2 * hgd), conv_col0=4 * d, seq_rows=s, dg=dg)

    gs_full = jnp.concatenate([jnp.broadcast_to(gs_meta[None], (b, CHUNK, LANES)),
                               gs_main.reshape(b, s, LANES)], axis=1)
    lane_g = (2 * hm, LANES - 2 * hm - hgd)
    prm = jnp.pad(jnp.stack([_pad_lanes(b_igate[0].astype(F32)), _pad_lanes(b_fgate[0].astype(F32)),
                             jnp.pad(a_log[0].astype(F32), lane_g), jnp.pad(dt_bias[0].astype(F32), lane_g)]),
                  ((0, SUBLANES - 4), (0, 0)))
    colm, colg, small = _gates(gs_full, prm, hm=hm, hgd=hgd, hg=hg)
    erow = small[:, 0, :, 0:hm].reshape(b, n_chunks + 1, CHUNK, hm).transpose(0, 1, 3, 2)
    grow = small[:, 1:3, :, 2 * hm:2 * hm + hgd].reshape(b, 2, n_chunks + 1, CHUNK, groups, hg)
    grow = grow.transpose(0, 2, 4, 1, 5, 3).reshape(b, n_chunks + 1, groups, 2, hg // _PACK, _PACK * CHUNK)

    dqk = d // 2 // hm
    dv = d // hm
    c0 = jnp.zeros((hm, dqk, dv), F32)
    n0 = jnp.zeros((hm, 1, dqk), F32)
    _, c1, n1 = _mlstm(p_meta, colm[:1, :CHUNK], erow[:1, :1], c0, n0, ml_norm_g[0], b=1, s=CHUNK, d=d, hm=hm)
    ya, _, _ = _mlstm(p_main, colm[:, CHUNK:], erow[:, 1:], c1[0], n1[0], ml_norm_g[0], b=b, s=s, d=d, hm=hm)

    s0 = jnp.zeros((hgd, dg, dg), F32)
    _, s1 = _gdn(p_meta, colg[:1, :CHUNK], grow[:1, :1], s0, gdn_norm_g[0],
                 b=1, s=CHUNK, d=d, hgd=hgd, hg=hg, qcol0=4 * d)
    yb, _ = _gdn(p_main, colg[:, CHUNK:], grow[:, 1:], s1[0], gdn_norm_g[0],
                 b=b, s=s, d=d, hgd=hgd, hg=hg, qcol0=4 * d)

    merged = _merge(ya, yb, wa, wb, p_main, gcol0=8 * d)
    return _outproj(merged, wo, x2d, norm_f_g).reshape(b, s, d)
```

```python
import functools

import jax
import jax.numpy as jnp
from jax import lax
from jax.experimental import pallas as pl
from jax.experimental.pallas import tpu as pltpu

CHUNK = 64
N_META = 16
N_PAD = CHUNK - N_META
EPS = 1e-6
NEG = -1e30
GATE_CAP = 15.0
LANES = 128
SUBLANES = 8
VMEM_LIMIT = 56 * 1024 * 1024
VMEM_LIMIT_OUTPROJ = 58 * 1024 * 1024
_CONV_SPLIT = 2
_MXU_TILE = 256
_PACK = _MXU_TILE // CHUNK

F32 = jnp.float32
BF16 = jnp.bfloat16

_P_C, _P_WINTER, _P_FLOOR, _P_WKF, _P_DECAY = range(5)


def _sigmoid(x):
    return 1.0 / (1.0 + jnp.exp(-x))


def _softplus(x):
    return jnp.maximum(x, 0.0) + jnp.log1p(jnp.exp(-jnp.abs(x)))


def _pick(n, cands):
    for c in cands:
        if c <= n and n % c == 0:
            return c
    return n


def _params(sem, vmem=None):
    return pltpu.CompilerParams(dimension_semantics=sem, vmem_limit_bytes=vmem)


_NT = (((1,), (1,)), ((), ()))


def _norm_gate_kernel(x_ref, g_ref, ws_ref, o_ref, gs_ref):
    x = x_ref[...].astype(F32)
    ms = jnp.mean(x * x, axis=-1, keepdims=True)
    hn = (x * lax.rsqrt(ms + EPS) * g_ref[...]).astype(o_ref.dtype)
    o_ref[...] = hn
    gs_ref[...] = lax.dot_general(hn, ws_ref[...].astype(hn.dtype), _NT, preferred_element_type=F32)


def _norm_gate(x2d, g, ws):
    m, d = x2d.shape
    tr = _pick(m, (256, 128, 64))
    return pl.pallas_call(
        _norm_gate_kernel,
        grid=(m // tr,),
        in_specs=[pl.BlockSpec((tr, d), lambda i: (i, 0)),
                  pl.BlockSpec((1, d), lambda i: (0, 0)),
                  pl.BlockSpec((LANES, d), lambda i: (0, 0))],
        out_specs=[pl.BlockSpec((tr, d), lambda i: (i, 0)),
                   pl.BlockSpec((tr, LANES), lambda i: (i, 0))],
        out_shape=[jax.ShapeDtypeStruct((m, d), BF16),
                   jax.ShapeDtypeStruct((m, LANES), F32)],
        compiler_params=_params(("parallel",)),
        name="norm_gate",
    )(x2d, g.reshape(1, d).astype(F32), ws)


def _inproj_kernel(h_ref, hm_ref, cw_ref, wt_hbm, p_ref, pm_ref, wstage, wbf, tail_scr, mtail_scr, sem,
                   *, tn, nj, seg_tiles, seg_skip, conv_j0, conv_nj, seq_tiles, dg, scale):
    j = pl.program_id(0)
    i = pl.program_id(1)
    hr = tail_scr.shape[0]
    is_conv = jnp.logical_and(j >= conv_j0, j < conv_j0 + 3 * conv_nj)
    is_v = j >= conv_j0 + 2 * conv_nj
    qscale = jnp.where(j < conv_j0 + conv_nj, scale, 1.0).astype(F32)

    def tile_copy(jj):
        off = jj * tn
        for tiles, skip in zip(seg_tiles, seg_skip):
            off = off + jnp.where(jj >= tiles, skip, 0)
        return pltpu.make_async_copy(wt_hbm.at[pl.ds(pl.multiple_of(off, 8), tn), :], wstage, sem)

    def conv_store(o_ref, acc, halo):
        m = acc.shape[0]
        w = cw_ref[...]
        ntap = w.shape[0]
        ext = jnp.concatenate([halo, acc], axis=0)
        y = w[ntap - 1:ntap, :] * acc
        for tap in range(1, ntap):
            y = y + w[ntap - 1 - tap:ntap - tap, :] * ext[hr - tap:hr - tap + m]
        a = y * _sigmoid(y)
        for hd in range(tn // dg):
            seg = a[:, hd * dg:(hd + 1) * dg]
            ss = jnp.sum(seg * seg, axis=1, keepdims=True)
            fac = jnp.where(is_v, 1.0, lax.rsqrt(ss + EPS) * qscale)
            o_ref[:, hd * dg:(hd + 1) * dg] = (seg * fac).astype(o_ref.dtype)

    @pl.when(i == 0)
    def _():
        @pl.when(j == 0)
        def _():
            tile_copy(j).start()

        tile_copy(j).wait()

        def cast_rows(s, carry):
            rows = pl.ds(pl.multiple_of(s * CHUNK, CHUNK), CHUNK)
            wbf[rows, :] = wstage[rows, :].astype(wbf.dtype)
            return carry

        lax.fori_loop(0, tn // CHUNK, cast_rows, 0)

        @pl.when(j + 1 < nj)
        def _():
            tile_copy(j + 1).start()

        accm = lax.dot_general(hm_ref[...], wbf[...], _NT, preferred_element_type=F32)
        mm = accm.shape[0]

        @pl.when(is_conv)
        def _():
            mtail_scr[...] = accm[mm - hr:mm]
            conv_store(pm_ref, accm, jnp.zeros((hr, tn), F32))

        @pl.when(jnp.logical_not(is_conv))
        def _():
            pm_ref[...] = accm.astype(pm_ref.dtype)

    @pl.when(is_conv)
    def _():
        tm = h_ref.shape[0]
        sub = tm // _CONV_SPLIT
        halo = jnp.where(i % seq_tiles == 0, mtail_scr[...], tail_scr[...])
        accs = []
        for part in range(_CONV_SPLIT):
            accs.append(lax.dot_general(h_ref[part * sub:(part + 1) * sub, :], wbf[...], _NT,
                                        preferred_element_type=F32))
            if part:
                conv_store(p_ref.at[(part - 1) * sub:part * sub, :], accs[part - 1], halo)
                halo = accs[part - 1][sub - hr:]
        conv_store(p_ref.at[tm - sub:tm, :], accs[-1], halo)
        tail_scr[...] = accs[-1][sub - hr:]

    @pl.when(jnp.logical_not(is_conv))
    def _():
        p_ref[...] = lax.dot_general(h_ref[...], wbf[...], _NT, preferred_element_type=F32).astype(p_ref.dtype)


def _inproj(hn, hn_meta, wt, conv_w, *, n_out, seg_starts, seg_skip, conv_col0, seq_rows, dg):
    m, d = hn.shape
    mm = hn_meta.shape[0]
    tm = _pick(seq_rows, (1024, 512, 256, 128, 64))
    tn = _pick(n_out, (1024, 512, 256, 128))
    assert all(st % tn == 0 for st in seg_starts) and all(sk % 8 == 0 for sk in seg_skip)
    assert m % tm == 0 and conv_col0 % tn == 0 and d % tn == 0 and tn % dg == 0 and seq_rows % tm == 0
    nj = n_out // tn
    conv_j0, conv_nj = conv_col0 // tn, d // tn
    return pl.pallas_call(
        functools.partial(_inproj_kernel, tn=tn, nj=nj, seg_tiles=tuple(st // tn for st in seg_starts),
                          seg_skip=tuple(seg_skip), conv_j0=conv_j0, conv_nj=conv_nj, seq_tiles=seq_rows // tm,
                          dg=dg, scale=float(dg) ** -0.5),
        grid=(nj, m // tm),
        in_specs=[pl.BlockSpec((tm, d), lambda j, i: (i, 0)),
                  pl.BlockSpec((mm, d), lambda j, i: (0, 0)),
                  pl.BlockSpec((conv_w.shape[0], tn), lambda j, i: (0, jnp.clip(j - conv_j0, 0, 3 * conv_nj - 1))),
                  pl.BlockSpec(memory_space=pl.ANY)],
        out_specs=[pl.BlockSpec((tm, tn), lambda j, i: (i, j)),
                   pl.BlockSpec((mm, tn), lambda j, i: (0, j))],
        out_shape=[jax.ShapeDtypeStruct((m, n_out), BF16),
                   jax.ShapeDtypeStruct((mm, n_out), BF16)],
        scratch_shapes=[pltpu.VMEM((tn, d), F32), pltpu.VMEM((tn, d), BF16),
                        pltpu.VMEM((SUBLANES, tn), F32), pltpu.VMEM((SUBLANES, tn), F32),
                        pltpu.SemaphoreType.DMA(())],
        compiler_params=_params(("arbitrary", "arbitrary"), VMEM_LIMIT),
        name="inproj",
    )(hn, hn_meta, conv_w.astype(F32), wt)


def _cumsum_rows(x, rowi):
    for d in (1, 2, 4, 8, 16, 32):
        x = x + jnp.where(rowi >= d, pltpu.roll(x, d, axis=0), 0.0)
    return x


def _cummax_rows(x, rowi):
    for d in (1, 2, 4, 8, 16, 32):
        x = jnp.maximum(x, jnp.where(rowi >= d, pltpu.roll(x, d, axis=0), NEG))
    return x


def _pack_lanes(vals, src0, width, lane):
    out = None
    for q, v in enumerate(vals):
        shift = (q * width - src0) % LANES
        r = pltpu.roll(v, shift, axis=1) if shift else v
        out = r if out is None else jnp.where(lane // width == q, r, out)
    return out


def _gates_kernel(gin_ref, prm_ref, colm_ref, colg_ref, small_ref, m_ref, *, tc, hm, hgd, hg):
    t = pl.program_id(1)

    @pl.when(t == 0)
    def _():
        m_ref[...] = jnp.zeros_like(m_ref)

    rowi = lax.broadcasted_iota(jnp.int32, (CHUNK, LANES), 0)
    lane = lax.broadcasted_iota(jnp.int32, (CHUNK, LANES), 1)
    b_i = prm_ref[0:1, :]
    b_f = prm_ref[1:2, :]
    neg_a = -jnp.exp(prm_ref[2:3, :])
    dtb = prm_ref[3:4, :]
    last = CHUNK - 1
    gsrc = 2 * hm
    for ci in range(tc):
        rows = slice(ci * CHUNK, (ci + 1) * CHUNK)
        valid = ((t * tc + ci) * CHUNK + rowi) >= N_PAD
        x = gin_ref[0, rows, :]
        m_f = pltpu.roll(x, LANES - hm, axis=1)
        g_b = pltpu.roll(x, LANES - hgd, axis=1)

        i_pre = GATE_CAP * jnp.tanh((x + b_i) / GATE_CAP)
        f_pre = GATE_CAP * jnp.tanh((m_f + b_f) / GATE_CAP)
        i_pre = jnp.where(valid, i_pre, NEG)
        logf = jnp.where(valid, -_softplus(-f_pre), 0.0)
        bcum = _cumsum_rows(logf, rowi)
        e = i_pre - bcum
        cm = _cummax_rows(e, rowi)
        gtot = bcum[last:last + 1, :]
        m_old = m_ref[0:1, :]
        m_new = jnp.maximum(gtot + m_old, gtot + cm[last:last + 1, :])
        c = jnp.maximum(cm, m_old)
        decay = jnp.broadcast_to(jnp.exp(gtot + m_old - m_new), (CHUNK, LANES))
        colm_ref[0, rows, :] = _pack_lanes(
            [c, jnp.exp(m_old - c), jnp.exp(-(bcum + c)), jnp.exp(gtot + e - m_new), decay], 0, hm, lane)
        small_ref[0, 0, rows, :] = e
        m_ref[...] = jnp.broadcast_to(m_new, m_ref.shape)

        g = jnp.where(valid, neg_a * _softplus(x + dtb), 0.0)
        beta = jnp.where(valid, _sigmoid(g_b), 0.0)
        gc = _cumsum_rows(g, rowi)
        gl = gc[last:last + 1, :]
        small_ref[0, 1, rows, :] = gc
        small_ref[0, 2, rows, :] = beta
        gvals = [gc, beta, jnp.exp(gc), jnp.exp(gl - gc), jnp.broadcast_to(jnp.exp(gl), (CHUNK, LANES))]
        for gi in range(hgd // hg):
            colg_ref[0, rows, gi * LANES:(gi + 1) * LANES] = _pack_lanes(gvals, gsrc + gi * hg, hg, lane)


def _gates(gs, prm, *, hm, hgd, hg):
    b, s_full, _ = gs.shape
    n_full = s_full // CHUNK
    groups = hgd // hg
    tc = _pick(n_full, (3, 4, 2, 1))
    r = tc * CHUNK
    return pl.pallas_call(
        functools.partial(_gates_kernel, tc=tc, hm=hm, hgd=hgd, hg=hg),
        grid=(b, n_full // tc),
        in_specs=[pl.BlockSpec((1, r, LANES), lambda i, t: (i, t, 0)),
                  pl.BlockSpec((SUBLANES, LANES), lambda i, t: (0, 0))],
        out_specs=[pl.BlockSpec((1, r, LANES), lambda i, t: (i, t, 0)),
                   pl.BlockSpec((1, r, groups * LANES), lambda i, t: (i, t, 0)),
                   pl.BlockSpec((1, 3, r, LANES), lambda i, t: (i, 0, t, 0))],
        out_shape=[jax.ShapeDtypeStruct((b, s_full, LANES), F32),
                   jax.ShapeDtypeStruct((b, s_full, groups * LANES), F32),
                   jax.ShapeDtypeStruct((b, 3, s_full, LANES), F32)],
        scratch_shapes=[pltpu.VMEM((SUBLANES, LANES), F32)],
        compiler_params=_params(("arbitrary", "arbitrary")),
        name="gates",
    )(gs, prm)


def _mlstm_kernel(q_ref, k_ref, v_ref, o_ref, z_ref, colm_ref, erow_ref, c0_ref, n0_ref, g_ref,
                  ya_ref, c_ref, n_ref, *, t_chunks, hm, dqk, dv, scale):
    @pl.when(pl.program_id(1) == 0)
    def _():
        c_ref[...] = c0_ref[...]
        n_ref[...] = n0_ref[...]

    ri = lax.broadcasted_iota(jnp.int32, (CHUNK, CHUNK), 0)
    ci = lax.broadcasted_iota(jnp.int32, (CHUNK, CHUNK), 1)
    causal = ci <= ri

    def chunk_body(c, carry):
        r0 = pl.multiple_of(c * CHUNK, CHUNK)
        rows = pl.ds(r0, CHUNK)
        colm = colm_ref[0, rows, :]
        for h in range(hm):
            qs = slice(h * dqk, (h + 1) * dqk)
            vs = slice(h * dv, (h + 1) * dv)
            q = q_ref[rows, qs]
            k = k_ref[rows, qs]
            v = v_ref[rows, vs]
            cc = colm[:, _P_C * hm + h:_P_C * hm + h + 1]
            winter = colm[:, _P_WINTER * hm + h:_P_WINTER * hm + h + 1]
            floor = colm[:, _P_FLOOR * hm + h:_P_FLOOR * hm + h + 1]
            wkf = colm[:, _P_WKF * hm + h:_P_WKF * hm + h + 1]
            decay = colm[0:1, _P_DECAY * hm + h:_P_DECAY * hm + h + 1]
            e_row = erow_ref[0, c, h:h + 1, :]

            qk = lax.dot_general(q, k, (((1,), (1,)), ((), ())), preferred_element_type=F32) * scale
            s = qk * jnp.exp(jnp.where(causal, e_row - cc, NEG))
            den_intra = jnp.sum(s, axis=1, keepdims=True)
            c_st = c_ref[h]
            n_st = n_ref[h]
            num = winter * jnp.dot(q, c_st.astype(BF16), preferred_element_type=F32) + jnp.dot(
                s.astype(BF16), v, preferred_element_type=F32)
            qn = jnp.sum(q.astype(F32) * n_st, axis=1, keepdims=True)
            den = winter * qn + den_intra
            hh = num * (1.0 / jnp.maximum(jnp.abs(den), floor))
            ms = jnp.mean(hh * hh, axis=1, keepdims=True)
            hn = hh * lax.rsqrt(ms + EPS) * g_ref[:, vs]
            og = o_ref[rows, vs].astype(F32)
            zg = z_ref[rows, vs].astype(F32)
            ya_ref[rows, vs] = (hn * _sigmoid(og) * (zg * _sigmoid(zg))).astype(ya_ref.dtype)

            wk = k.astype(F32) * (wkf * scale)
            upd = lax.dot_general(wk.astype(BF16), v, (((0,), (0,)), ((), ())), preferred_element_type=F32)
            c_ref[h] = decay * c_st + upd
            n_ref[h] = decay * n_st + jnp.sum(wk, axis=0, keepdims=True)
        return carry

    lax.fori_loop(0, t_chunks, chunk_body, 0)


def _mlstm(p, colm, erow, c0, n0, g, *, b, s, d, hm):
    dqk = d // 2 // hm
    dv = d // hm
    t_chunks = _pick(s // CHUNK, (4, 2, 1))
    rb = t_chunks * CHUNK
    nt = s // rb
    hw = d // 2
    row = lambda bi, t: bi * nt + t
    return pl.pallas_call(
        functools.partial(_mlstm_kernel, t_chunks=t_chunks, hm=hm, dqk=dqk, dv=dv, scale=float(dqk) ** -0.5),
        grid=(b, nt),
        in_specs=[pl.BlockSpec((rb, hw), lambda bi, t: (row(bi, t), 0)),
                  pl.BlockSpec((rb, hw), lambda bi, t: (row(bi, t), 1)),
                  pl.BlockSpec((rb, d), lambda bi, t: (row(bi, t), 1)),
                  pl.BlockSpec((rb, d), lambda bi, t: (row(bi, t), 2)),
                  pl.BlockSpec((rb, d), lambda bi, t: (row(bi, t), 3)),
                  pl.BlockSpec((1, rb, LANES), lambda bi, t: (bi, t, 0)),
                  pl.BlockSpec((1, t_chunks, hm, CHUNK), lambda bi, t: (bi, t, 0, 0)),
                  pl.BlockSpec((hm, dqk, dv), lambda bi, t: (0, 0, 0)),
                  pl.BlockSpec((hm, 1, dqk), lambda bi, t: (0, 0, 0)),
                  pl.BlockSpec((1, d), lambda bi, t: (0, 0))],
        out_specs=[pl.BlockSpec((rb, d), lambda bi, t: (row(bi, t), 0)),
                   pl.BlockSpec((None, hm, dqk, dv), lambda bi, t: (bi, 0, 0, 0)),
                   pl.BlockSpec((None, hm, 1, dqk), lambda bi, t: (bi, 0, 0, 0))],
        out_shape=[jax.ShapeDtypeStruct((b * s, d), BF16),
                   jax.ShapeDtypeStruct((b, hm, dqk, dv), F32),
                   jax.ShapeDtypeStruct((b, hm, 1, dqk), F32)],
        compiler_params=_params(("arbitrary", "arbitrary"), VMEM_LIMIT),
        name="mlstm",
    )(p, p, p, p, p, colm, erow, c0, n0, g.reshape(1, d).astype(F32))


def _gdn_kernel(q_ref, k_ref, v_ref, z_ref, colg_ref, grow_ref, s0_ref, g_ref, yb_ref, s_ref, qk_scr, u_scr, wq_scr,
                *, t_chunks, hg, dg):
    @pl.when(pl.program_id(2) == 0)
    def _():
        s_ref[...] = s0_ref[...]

    npk = hg // _PACK
    side = _PACK * CHUNK
    row = lax.broadcasted_iota(jnp.int32, (CHUNK, side), 0)
    lane = lax.broadcasted_iota(jnp.int32, (CHUNK, side), 1)
    lj = lane % CHUNK
    lgrp = lane // CHUNK
    incl = lj <= row
    strict = lj < row
    eye = (lj == row).astype(F32)
    bd_mask = (lax.broadcasted_iota(jnp.int32, (side, side), 0) // CHUNK
               == lax.broadcasted_iota(jnp.int32, (side, side), 1) // CHUNK)
    kgrp = lax.broadcasted_iota(jnp.int32, (CHUNK, _PACK * dg), 1) // dg
    tdims = (((0,), (0,)), ((), ()))

    def blockdiag(x):
        return jnp.where(bd_mask, jnp.concatenate([x] * _PACK, axis=0), jnp.zeros((), x.dtype))

    def spread(colg, j, g):
        base = j * hg + g * _PACK
        acc = jnp.broadcast_to(colg[:, base:base + 1], (CHUNK, side))
        for hq in range(1, _PACK):
            acc = jnp.where(lgrp == hq, colg[:, base + hq:base + hq + 1], acc)
        return acc

    units = [(c, g) for c in range(t_chunks) for g in range(npk)]
    us = range(len(units))
    rws = [slice(c * CHUNK, (c + 1) * CHUNK) for c, _ in units]
    lns = [slice(g * _PACK * dg, (g + 1) * _PACK * dg) for _, g in units]
    colgs = [colg_ref[0, rws[u], :] for u in us]
    k4 = [k_ref[rws[u], lns[u]] for u in us]
    q4 = [q_ref[rws[u], lns[u]] for u in us]
    kbd = [jnp.concatenate([jnp.where(kgrp == hq, k4[u], jnp.zeros((), k4[u].dtype)) for hq in range(_PACK)], axis=0)
           for u in us]
    qkkk = [lax.dot_general(jnp.concatenate([q4[u], k4[u]], axis=0), kbd[u], _NT, preferred_element_type=F32)
            for u in us]
    dec = [jnp.exp(jnp.where(incl, spread(colgs[u], 0, units[u][1])
                             - grow_ref[0, units[u][0], 0, 0, units[u][1]:units[u][1] + 1, :], NEG)) for u in us]
    for u in us:
        qk_scr[units[u][0], units[u][1]] = (qkkk[u][:CHUNK] * dec[u]).astype(qk_scr.dtype)
    a = [jnp.where(strict, qkkk[u][CHUNK:] * dec[u], 0.0) * spread(colgs[u], 1, units[u][1]) for u in us]
    tm = [eye - a[u] for u in us]
    ab = [a[u].astype(BF16) for u in us]
    pw = [jnp.dot(ab[u], blockdiag(ab[u]), preferred_element_type=F32) for u in us]
    for _ in range(4):
        pb = [pw[u].astype(BF16) for u in us]
        both = [jnp.dot(jnp.concatenate([tm[u].astype(BF16), pb[u]], axis=0), blockdiag(pb[u]),
                        preferred_element_type=F32) for u in us]
        tm = [tm[u] + both[u][:CHUNK] for u in us]
        pw = [both[u][CHUNK:] for u in us]
    tb = [((tm[u] + jnp.dot(tm[u].astype(BF16), blockdiag(pw[u].astype(BF16)), preferred_element_type=F32))
           * grow_ref[0, units[u][0], 0, 1, units[u][1]:units[u][1] + 1, :]).astype(BF16) for u in us]

    def pad_rows(x, hq):
        parts = []
        if hq:
            parts.append(jnp.zeros((hq * CHUNK, x.shape[1]), x.dtype))
        parts.append(x)
        if hq < _PACK - 1:
            parts.append(jnp.zeros(((_PACK - 1 - hq) * CHUNK, x.shape[1]), x.dtype))
        return jnp.concatenate(parts, axis=0)

    items = [(u, hq) for u in us for hq in range(_PACK)]
    heads = [units[u][1] * _PACK + hq for u, hq in items]
    hsl = [slice(h * dg, (h + 1) * dg) for h in heads]
    expgc = [colgs[u][:, 2 * hg + h:2 * hg + h + 1] for (u, _), h in zip(items, heads)]
    rhs = [jnp.concatenate([v_ref[rws[u], hsl[i]], (k_ref[rws[u], hsl[i]].astype(F32) * expgc[i]).astype(BF16)], axis=1)
           for i, (u, _) in enumerate(items)]
    uw = [jnp.dot(tb[u], pad_rows(rhs[i], hq), preferred_element_type=F32) for i, (u, hq) in enumerate(items)]
    for i, (u, _) in enumerate(items):
        c, h = units[u][0], heads[i]
        u_scr[c, h] = uw[i][:, :dg]
        wq_scr[c, h, 0:CHUNK, :] = uw[i][:, dg:].astype(wq_scr.dtype)
        wq_scr[c, h, CHUNK:2 * CHUNK, :] = (q_ref[rws[u], hsl[i]].astype(F32) * expgc[i]).astype(wq_scr.dtype)

    def chunk_body(c, carry):
        r0 = pl.multiple_of(c * CHUNK, CHUNK)
        rows = pl.ds(r0, CHUNK)
        colg = colg_ref[0, rows, :]
        hs = range(hg)
        sls = [slice(h * dg, (h + 1) * dg) for h in hs]
        col = lambda j, h: colg[:, j * hg + h:j * hg + h + 1]
        s_st = [s_ref[h] for h in hs]
        s_b = [s_st[h].astype(BF16) for h in hs]
        wqs = [jnp.dot(wq_scr[c, h], s_b[h], preferred_element_type=F32) for h in hs]
        v_nb = [(u_scr[c, h] - wqs[h][:CHUNK]).astype(BF16) for h in hs]
        o = [wqs[h][CHUNK:] + jnp.dot(qk_scr[c, h // _PACK], pad_rows(v_nb[h], h % _PACK), preferred_element_type=F32)
             for h in hs]
        k_dec = [(k_ref[rows, sls[h]].astype(F32) * col(3, h)).astype(BF16) for h in hs]
        for h in hs:
            expgl = colg[0:1, 4 * hg + h:4 * hg + h + 1]
            s_ref[h] = expgl * s_st[h] + lax.dot_general(k_dec[h], v_nb[h], tdims, preferred_element_type=F32)
        for h in hs:
            ms = jnp.mean(o[h] * o[h], axis=1, keepdims=True)
            zg = z_ref[rows, sls[h]].astype(F32)
            yb_ref[rows, sls[h]] = (o[h] * lax.rsqrt(ms + EPS) * g_ref[...] * (zg * _sigmoid(zg))).astype(yb_ref.dtype)
        return carry

    lax.fori_loop(0, t_chunks, chunk_body, 0, unroll=2 if t_chunks % 2 == 0 else 1)


def _gdn(p, colg, grow, s0, g, *, b, s, d, hgd, hg, qcol0):
    dg = d // hgd
    groups = hgd // hg
    gw = hg * dg
    npk = hg // _PACK
    side = _PACK * CHUNK
    t_chunks = _pick(s // CHUNK, (8, 4, 2, 1))
    rb = t_chunks * CHUNK
    nt = s // rb
    qb = qcol0 // gw
    row = lambda bi, t: bi * nt + t
    return pl.pallas_call(
        functools.partial(_gdn_kernel, t_chunks=t_chunks, hg=hg, dg=dg),
        grid=(b, groups, nt),
        in_specs=[pl.BlockSpec((rb, gw), lambda bi, gi, t: (row(bi, t), qb + gi)),
                  pl.BlockSpec((rb, gw), lambda bi, gi, t: (row(bi, t), qb + groups + gi)),
                  pl.BlockSpec((rb, gw), lambda bi, gi, t: (row(bi, t), qb + 2 * groups + gi)),
                  pl.BlockSpec((rb, gw), lambda bi, gi, t: (row(bi, t), qb + 3 * groups + gi)),
                  pl.BlockSpec((1, rb, LANES), lambda bi, gi, t: (bi, t, gi)),
                  pl.BlockSpec((1, t_chunks, 1, 2, npk, side), lambda bi, gi, t: (bi, t, gi, 0, 0, 0)),
                  pl.BlockSpec((hg, dg, dg), lambda bi, gi, t: (gi, 0, 0)),
                  pl.BlockSpec((1, dg), lambda bi, gi, t: (0, 0))],
        out_specs=[pl.BlockSpec((rb, gw), lambda bi, gi, t: (row(bi, t), gi)),
                   pl.BlockSpec((None, hg, dg, dg), lambda bi, gi, t: (bi, gi, 0, 0))],
        out_shape=[jax.ShapeDtypeStruct((b * s, d), BF16),
                   jax.ShapeDtypeStruct((b, hgd, dg, dg), F32)],
        scratch_shapes=[pltpu.VMEM((t_chunks, npk, CHUNK, side), BF16),
                        pltpu.VMEM((t_chunks, hg, CHUNK, dg), F32),
                        pltpu.VMEM((t_chunks, hg, 2 * CHUNK, dg), BF16)],
        compiler_params=_params(("arbitrary", "arbitrary", "arbitrary"), VMEM_LIMIT),
        name="gdn",
    )(p, p, p, p, colg, grow, s0, g.reshape(1, dg).astype(F32))


def _merge_kernel(ya_ref, yb_ref, wa_ref, wb_ref, ga_ref, gb_ref, o_ref):
    pa = jnp.dot(ya_ref[...], wa_ref[...], preferred_element_type=F32)
    pb = jnp.dot(yb_ref[...], wb_ref[...], preferred_element_type=F32)
    o_ref[...] = (_sigmoid(ga_ref[...].astype(F32)) * pa + _sigmoid(gb_ref[...].astype(F32)) * pb).astype(o_ref.dtype)


def _merge(ya, yb, wa, wb, p, *, gcol0):
    m, d = ya.shape
    tm = _pick(m, (256, 128, 64))
    tn = _pick(d, (1024, 512, 256, 128))
    ga0 = gcol0 // tn
    gb0 = (gcol0 + d) // tn
    return pl.pallas_call(
        _merge_kernel,
        grid=(d // tn, m // tm),
        in_specs=[pl.BlockSpec((tm, d), lambda j, i: (i, 0)),
                  pl.BlockSpec((tm, d), lambda j, i: (i, 0)),
                  pl.BlockSpec((d, tn), lambda j, i: (0, j)),
                  pl.BlockSpec((d, tn), lambda j, i: (0, j)),
                  pl.BlockSpec((tm, tn), lambda j, i: (i, ga0 + j)),
                  pl.BlockSpec((tm, tn), lambda j, i: (i, gb0 + j))],
        out_specs=pl.BlockSpec((tm, tn), lambda j, i: (i, j)),
        out_shape=jax.ShapeDtypeStruct((m, d), BF16),
        compiler_params=_params(("parallel", "parallel"), VMEM_LIMIT),
        name="merge",
    )(ya, yb, wa, wb, p, p)


def _outproj_kernel(m_ref, w_ref, x_ref, g_ref, o_ref, ss_ref, *, tn, nj, d):
    j = pl.program_id(1)
    y = x_ref[...] + jnp.dot(m_ref[...], w_ref[...], preferred_element_type=F32)
    part = jnp.sum(y * y, axis=1, keepdims=True)

    @pl.when(j == 0)
    def _():
        ss_ref[...] = part

    @pl.when(j > 0)
    def _():
        ss_ref[...] += part

    for jj in range(nj):
        @pl.when(j == jj)
        def _(jj=jj):
            o_ref[:, jj * tn:(jj + 1) * tn] = y

    @pl.when(j == nj - 1)
    def _():
        def norm_rows(s, carry):
            rows = pl.ds(pl.multiple_of(s * CHUNK, CHUNK), CHUNK)
            o_ref[rows, :] = o_ref[rows, :] * lax.rsqrt(ss_ref[rows, :] / d + EPS) * g_ref[...]
            return carry

        lax.fori_loop(0, o_ref.shape[0] // CHUNK, norm_rows, 0)


def _outproj(merged, wo, x2d, g):
    m, d = merged.shape
    tm = _pick(m, (1024, 512, 256, 128, 64))
    tn = _pick(d, (512, 256, 128))
    nj = d // tn
    return pl.pallas_call(
        functools.partial(_outproj_kernel, tn=tn, nj=nj, d=d),
        grid=(m // tm, nj),
        in_specs=[pl.BlockSpec((tm, d), lambda i, j: (i, 0), pipeline_mode=pl.Buffered(1)),
                  pl.BlockSpec((d, tn), lambda i, j: (0, j)),
                  pl.BlockSpec((tm, tn), lambda i, j: (i, j)),
                  pl.BlockSpec((1, d), lambda i, j: (0, 0))],
        out_specs=pl.BlockSpec((tm, d), lambda i, j: (i, 0)),
        out_shape=jax.ShapeDtypeStruct((m, d), F32),
        scratch_shapes=[pltpu.VMEM((tm, 1), F32)],
        compiler_params=_params(("parallel", "arbitrary"), VMEM_LIMIT_OUTPROJ),
        name="outproj",
    )(merged, wo, x2d, g.reshape(1, d).astype(F32))


def _pad_lanes(a, width=LANES):
    return jnp.pad(a, [(0, 0)] * (a.ndim - 1) + [(0, width - a.shape[-1])])


def kernel(x, meta, norm_in_g, w_in, b_igate, b_fgate, ml_norm_g, conv_w, a_log, dt_bias,
           gdn_norm_g, w_proj_a, w_proj_b, w_out, norm_f_g):
    assert norm_in_g.shape[0] == 1, "single-layer block"
    b, s, d = x.shape
    hm = b_igate.shape[-1]
    hgd = a_log.shape[-1]
    dg = d // hgd
    hg = min(8, hgd)
    groups = hgd // hg
    assert s % CHUNK == 0 and meta.shape[0] == N_META and 2 * hm + 2 * hgd <= LANES
    n_chunks = s // CHUNK
    s_full = s + CHUNK

    wt = jnp.swapaxes(w_in[0], 0, 1)
    g0 = 4 * d
    g1 = g0 + 2 * hm
    g2 = g1 + 4 * d
    g3 = g2 + 2 * hgd
    w_gate = jnp.pad(jnp.concatenate([wt[g0:g1], wt[g2:g3]], axis=0), ((0, LANES - 2 * hm - 2 * hgd), (0, 0)))
    wa = w_proj_a[0].astype(BF16)
    wb = w_proj_b[0].astype(BF16)
    wo = w_out[0].astype(BF16)

    x2d = x.reshape(b * s, d)
    h0 = jnp.concatenate([jnp.zeros((N_PAD, d), x.dtype), meta.astype(x.dtype)], axis=0)
    hn_main, gs_main = _norm_gate(x2d, norm_in_g[0], w_gate)
    hn_meta, gs_meta = _norm_gate(h0, norm_in_g[0], w_gate)
    p_main, p_meta = _inproj(hn_main, hn_meta, wt, conv_w[0], n_out=10 * d, seg_starts=(4 * d, 8 * d),
                             seg_skip=(2 * hm, 2 * hgd), conv_col0=4 * d, seq_rows=s, dg=dg)

    gs_full = jnp.concatenate([jnp.broadcast_to(gs_meta[None], (b, CHUNK, LANES)),
                               gs_main.reshape(b, s, LANES)], axis=1)
    lane_g = (2 * hm, LANES - 2 * hm - hgd)
    prm = jnp.pad(jnp.stack([_pad_lanes(b_igate[0].astype(F32)), _pad_lanes(b_fgate[0].astype(F32)),
                             jnp.pad(a_log[0].astype(F32), lane_g), jnp.pad(dt_bias[0].astype(F32), lane_g)]),
                  ((0, SUBLANES - 4), (0, 0)))
    colm, colg, small = _gates(gs_full, prm, hm=hm, hgd=hgd, hg=hg)
    erow = small[:, 0, :, 0:hm].reshape(b, n_chunks + 1, CHUNK, hm).transpose(0, 1, 3, 2)
    grow = small[:, 1:3, :, 2 * hm:2 * hm + hgd].reshape(b, 2, n_chunks + 1, CHUNK, groups, hg)
    grow = grow.transpose(0, 2, 4, 1, 5, 3).reshape(b, n_chunks + 1, groups, 2, hg // _PACK, _PACK * CHUNK)

    dqk = d // 2 // hm
    dv = d // hm
    c0 = jnp.zeros((hm, dqk, dv), F32)
    n0 = jnp.zeros((hm, 1, dqk), F32)
    _, c1, n1 = _mlstm(p_meta, colm[:1, :CHUNK], erow[:1, :1], c0, n0, ml_norm_g[0], b=1, s=CHUNK, d=d, hm=hm)
    ya, _, _ = _mlstm(p_main, colm[:, CHUNK:], erow[:, 1:], c1[0], n1[0], ml_norm_g[0], b=b, s=s, d=d, hm=hm)

    s0 = jnp.zeros((hgd, dg, dg), F32)
    _, s1 = _gdn(p_meta, colg[:1, :CHUNK], grow[:1, :1], s0, gdn_norm_g[0],
                 b=1, s=CHUNK, d=d, hgd=hgd, hg=hg, qcol0=4 * d)
    yb, _ = _gdn(p_main, colg[:, CHUNK:], grow[:, 1:], s1[0], gdn_norm_g[0],
                 b=b, s=s, d=d, hgd=hgd, hg=hg, qcol0=4 * d)

    merged = _merge(ya, yb, wa, wb, p_main, gcol0=8 * d)
    return _outproj(merged, wo, x2d, norm_f_g).reshape(b, s, d)
```

```python
import functools

import jax
import jax.numpy as jnp
from jax import lax
from jax.experimental import pallas as pl
from jax.experimental.pallas import tpu as pltpu

CHUNK = 64
N_META = 16
N_PAD = CHUNK - N_META
EPS = 1e-6
NEG = -1e30
GATE_CAP = 15.0
LANES = 128
SUBLANES = 8
VMEM_LIMIT = 56 * 1024 * 1024
_CONV_SPLIT = 4
_MXU_TILE = 256
_PACK = _MXU_TILE // CHUNK

F32 = jnp.float32
BF16 = jnp.bfloat16

_P_C, _P_WINTER, _P_FLOOR, _P_WKF, _P_DECAY = range(5)


def _sigmoid(x):
    return 1.0 / (1.0 + jnp.exp(-x))


def _softplus(x):
    return jnp.maximum(x, 0.0) + jnp.log1p(jnp.exp(-jnp.abs(x)))


def _pick(n, cands):
    for c in cands:
        if c <= n and n % c == 0:
            return c
    return n


def _params(sem, vmem=None):
    return pltpu.CompilerParams(dimension_semantics=sem, vmem_limit_bytes=vmem)


_NT = (((1,), (1,)), ((), ()))


def _norm_gate_kernel(x_ref, g_ref, ws_ref, o_ref, gs_ref):
    x = x_ref[...].astype(F32)
    ms = jnp.mean(x * x, axis=-1, keepdims=True)
    hn = (x * lax.rsqrt(ms + EPS) * g_ref[...]).astype(o_ref.dtype)
    o_ref[...] = hn
    gs_ref[...] = lax.dot_general(hn, ws_ref[...].astype(hn.dtype), _NT, preferred_element_type=F32)


def _norm_gate(x2d, g, ws):
    m, d = x2d.shape
    tr = _pick(m, (256, 128, 64))
    return pl.pallas_call(
        _norm_gate_kernel,
        grid=(m // tr,),
        in_specs=[pl.BlockSpec((tr, d), lambda i: (i, 0)),
                  pl.BlockSpec((1, d), lambda i: (0, 0)),
                  pl.BlockSpec((LANES, d), lambda i: (0, 0))],
        out_specs=[pl.BlockSpec((tr, d), lambda i: (i, 0)),
                   pl.BlockSpec((tr, LANES), lambda i: (i, 0))],
        out_shape=[jax.ShapeDtypeStruct((m, d), BF16),
                   jax.ShapeDtypeStruct((m, LANES), F32)],
        compiler_params=_params(("parallel",)),
        name="norm_gate",
    )(x2d, g.reshape(1, d).astype(F32), ws)


def _inproj_kernel(h_ref, hm_ref, cw_ref, wt_hbm, p_ref, pm_ref, wstage, wbf, tail_scr, mtail_scr, sem,
                   *, tn, nj, seg_tiles, seg_skip, conv_j0, conv_nj, seq_tiles, dg, scale):
    j = pl.program_id(0)
    i = pl.program_id(1)
    hr = tail_scr.shape[0]
    is_conv = jnp.logical_and(j >= conv_j0, j < conv_j0 + 3 * conv_nj)
    is_v = j >= conv_j0 + 2 * conv_nj
    qscale = jnp.where(j < conv_j0 + conv_nj, scale, 1.0).astype(F32)

    def tile_copy(jj):
        off = jj * tn
        for tiles, skip in zip(seg_tiles, seg_skip):
            off = off + jnp.where(jj >= tiles, skip, 0)
        return pltpu.make_async_copy(wt_hbm.at[pl.ds(pl.multiple_of(off, 8), tn), :], wstage, sem)

    def conv_store(o_ref, acc, halo):
        m = acc.shape[0]
        w = cw_ref[...]
        ntap = w.shape[0]
        ext = jnp.concatenate([halo, acc], axis=0)
        y = w[ntap - 1:ntap, :] * acc
        for tap in range(1, ntap):
            y = y + w[ntap - 1 - tap:ntap - tap, :] * ext[hr - tap:hr - tap + m]
        a = y * _sigmoid(y)
        for hd in range(tn // dg):
            seg = a[:, hd * dg:(hd + 1) * dg]
            ss = jnp.sum(seg * seg, axis=1, keepdims=True)
            fac = jnp.where(is_v, 1.0, lax.rsqrt(ss + EPS) * qscale)
            o_ref[:, hd * dg:(hd + 1) * dg] = (seg * fac).astype(o_ref.dtype)

    @pl.when(i == 0)
    def _():
        @pl.when(j == 0)
        def _():
            tile_copy(j).start()

        tile_copy(j).wait()

        def cast_rows(s, carry):
            rows = pl.ds(pl.multiple_of(s * CHUNK, CHUNK), CHUNK)
            wbf[rows, :] = wstage[rows, :].astype(wbf.dtype)
            return carry

        lax.fori_loop(0, tn // CHUNK, cast_rows, 0)

        @pl.when(j + 1 < nj)
        def _():
            tile_copy(j + 1).start()

        accm = lax.dot_general(hm_ref[...], wbf[...], _NT, preferred_element_type=F32)
        mm = accm.shape[0]

        @pl.when(is_conv)
        def _():
            mtail_scr[...] = accm[mm - hr:mm]
            conv_store(pm_ref, accm, jnp.zeros((hr, tn), F32))

        @pl.when(jnp.logical_not(is_conv))
        def _():
            pm_ref[...] = accm.astype(pm_ref.dtype)

    @pl.when(is_conv)
    def _():
        tm = h_ref.shape[0]
        sub = tm // _CONV_SPLIT
        halo = jnp.where(i % seq_tiles == 0, mtail_scr[...], tail_scr[...])
        accs = []
        for part in range(_CONV_SPLIT):
            accs.append(lax.dot_general(h_ref[part * sub:(part + 1) * sub, :], wbf[...], _NT,
                                        preferred_element_type=F32))
            if part:
                conv_store(p_ref.at[(part - 1) * sub:part * sub, :], accs[part - 1], halo)
                halo = accs[part - 1][sub - hr:]
        conv_store(p_ref.at[tm - sub:tm, :], accs[-1], halo)
        tail_scr[...] = accs[-1][sub - hr:]

    @pl.when(jnp.logical_not(is_conv))
    def _():
        p_ref[...] = lax.dot_general(h_ref[...], wbf[...], _NT, preferred_element_type=F32).astype(p_ref.dtype)


def _inproj(hn, hn_meta, wt, conv_w, *, n_out, seg_starts, seg_skip, conv_col0, seq_rows, dg):
    m, d = hn.shape
    mm = hn_meta.shape[0]
    tm = _pick(seq_rows, (1024, 512, 256, 128, 64))
    tn = _pick(n_out, (1024, 512, 256, 128))
    assert all(st % tn == 0 for st in seg_starts) and all(sk % 8 == 0 for sk in seg_skip)
    assert m % tm == 0 and conv_col0 % tn == 0 and d % tn == 0 and tn % dg == 0 and seq_rows % tm == 0
    nj = n_out // tn
    conv_j0, conv_nj = conv_col0 // tn, d // tn
    return pl.pallas_call(
        functools.partial(_inproj_kernel, tn=tn, nj=nj, seg_tiles=tuple(st // tn for st in seg_starts),
                          seg_skip=tuple(seg_skip), conv_j0=conv_j0, conv_nj=conv_nj, seq_tiles=seq_rows // tm,
                          dg=dg, scale=float(dg) ** -0.5),
        grid=(nj, m // tm),
        in_specs=[pl.BlockSpec((tm, d), lambda j, i: (i, 0)),
                  pl.BlockSpec((mm, d), lambda j, i: (0, 0)),
                  pl.BlockSpec((conv_w.shape[0], tn), lambda j, i: (0, jnp.clip(j - conv_j0, 0, 3 * conv_nj - 1))),
                  pl.BlockSpec(memory_space=pl.ANY)],
        out_specs=[pl.BlockSpec((tm, tn), lambda j, i: (i, j)),
                   pl.BlockSpec((mm, tn), lambda j, i: (0, j))],
        out_shape=[jax.ShapeDtypeStruct((m, n_out), BF16),
                   jax.ShapeDtypeStruct((mm, n_out), BF16)],
        scratch_shapes=[pltpu.VMEM((tn, d), F32), pltpu.VMEM((tn, d), BF16),
                        pltpu.VMEM((SUBLANES, tn), F32), pltpu.VMEM((SUBLANES, tn), F32),
                        pltpu.SemaphoreType.DMA(())],
        compiler_params=_params(("arbitrary", "arbitrary"), VMEM_LIMIT),
        name="inproj",
    )(hn, hn_meta, conv_w.astype(F32), wt)


def _cumsum_rows(x, rowi):
    for d in (1, 2, 4, 8, 16, 32):
        x = x + jnp.where(rowi >= d, pltpu.roll(x, d, axis=0), 0.0)
    return x


def _cummax_rows(x, rowi):
    for d in (1, 2, 4, 8, 16, 32):
        x = jnp.maximum(x, jnp.where(rowi >= d, pltpu.roll(x, d, axis=0), NEG))
    return x


def _pack_lanes(vals, src0, width, lane):
    out = None
    for q, v in enumerate(vals):
        shift = (q * width - src0) % LANES
        r = pltpu.roll(v, shift, axis=1) if shift else v
        out = r if out is None else jnp.where(lane // width == q, r, out)
    return out


def _gates_kernel(gin_ref, prm_ref, colm_ref, colg_ref, small_ref, m_ref, *, tc, hm, hgd, hg):
    t = pl.program_id(1)

    @pl.when(t == 0)
    def _():
        m_ref[...] = jnp.zeros_like(m_ref)

    rowi = lax.broadcasted_iota(jnp.int32, (CHUNK, LANES), 0)
    lane = lax.broadcasted_iota(jnp.int32, (CHUNK, LANES), 1)
    b_i = prm_ref[0:1, :]
    b_f = prm_ref[1:2, :]
    neg_a = -jnp.exp(prm_ref[2:3, :])
    dtb = prm_ref[3:4, :]
    last = CHUNK - 1
    gsrc = 2 * hm
    for ci in range(tc):
        rows = slice(ci * CHUNK, (ci + 1) * CHUNK)
        valid = ((t * tc + ci) * CHUNK + rowi) >= N_PAD
        x = gin_ref[0, rows, :]
        m_f = pltpu.roll(x, LANES - hm, axis=1)
        g_b = pltpu.roll(x, LANES - hgd, axis=1)

        i_pre = GATE_CAP * jnp.tanh((x + b_i) / GATE_CAP)
        f_pre = GATE_CAP * jnp.tanh((m_f + b_f) / GATE_CAP)
        i_pre = jnp.where(valid, i_pre, NEG)
        logf = jnp.where(valid, -_softplus(-f_pre), 0.0)
        bcum = _cumsum_rows(logf, rowi)
        e = i_pre - bcum
        cm = _cummax_rows(e, rowi)
        gtot = bcum[last:last + 1, :]
        m_old = m_ref[0:1, :]
        m_new = jnp.maximum(gtot + m_old, gtot + cm[last:last + 1, :])
        c = jnp.maximum(cm, m_old)
        decay = jnp.broadcast_to(jnp.exp(gtot + m_old - m_new), (CHUNK, LANES))
        colm_ref[0, rows, :] = _pack_lanes(
            [c, jnp.exp(m_old - c), jnp.exp(-(bcum + c)), jnp.exp(gtot + e - m_new), decay], 0, hm, lane)
        small_ref[0, 0, rows, :] = e
        m_ref[...] = jnp.broadcast_to(m_new, m_ref.shape)

        g = jnp.where(valid, neg_a * _softplus(x + dtb), 0.0)
        beta = jnp.where(valid, _sigmoid(g_b), 0.0)
        gc = _cumsum_rows(g, rowi)
        gl = gc[last:last + 1, :]
        small_ref[0, 1, rows, :] = gc
        small_ref[0, 2, rows, :] = beta
        gvals = [gc, beta, jnp.exp(gc), jnp.exp(gl - gc), jnp.broadcast_to(jnp.exp(gl), (CHUNK, LANES))]
        for gi in range(hgd // hg):
            colg_ref[0, rows, gi * LANES:(gi + 1) * LANES] = _pack_lanes(gvals, gsrc + gi * hg, hg, lane)


def _gates(gs, prm, *, hm, hgd, hg):
    b, s_full, _ = gs.shape
    n_full = s_full // CHUNK
    groups = hgd // hg
    tc = _pick(n_full, (3, 4, 2, 1))
    r = tc * CHUNK
    return pl.pallas_call(
        functools.partial(_gates_kernel, tc=tc, hm=hm, hgd=hgd, hg=hg),
        grid=(b, n_full // tc),
        in_specs=[pl.BlockSpec((1, r, LANES), lambda i, t: (i, t, 0)),
                  pl.BlockSpec((SUBLANES, LANES), lambda i, t: (0, 0))],
        out_specs=[pl.BlockSpec((1, r, LANES), lambda i, t: (i, t, 0)),
                   pl.BlockSpec((1, r, groups * LANES), lambda i, t: (i, t, 0)),
                   pl.BlockSpec((1, 3, r, LANES), lambda i, t: (i, 0, t, 0))],
        out_shape=[jax.ShapeDtypeStruct((b, s_full, LANES), F32),
                   jax.ShapeDtypeStruct((b, s_full, groups * LANES), F32),
                   jax.ShapeDtypeStruct((b, 3, s_full, LANES), F32)],
        scratch_shapes=[pltpu.VMEM((SUBLANES, LANES), F32)],
        compiler_params=_params(("arbitrary", "arbitrary")),
        name="gates",
    )(gs, prm)


def _mlstm_kernel(q_ref, k_ref, v_ref, o_ref, z_ref, colm_ref, erow_ref, c0_ref, n0_ref, g_ref,
                  ya_ref, c_ref, n_ref, *, t_chunks, hm, dqk, dv, scale):
    @pl.when(pl.program_id(1) == 0)
    def _():
        c_ref[...] = c0_ref[...]
        n_ref[...] = n0_ref[...]

    ri = lax.broadcasted_iota(jnp.int32, (CHUNK, CHUNK), 0)
    ci = lax.broadcasted_iota(jnp.int32, (CHUNK, CHUNK), 1)
    causal = ci <= ri

    def chunk_body(c, carry):
        r0 = pl.multiple_of(c * CHUNK, CHUNK)
        rows = pl.ds(r0, CHUNK)
        colm = colm_ref[0, rows, :]
        for h in range(hm):
            qs = slice(h * dqk, (h + 1) * dqk)
            vs = slice(h * dv, (h + 1) * dv)
            q = q_ref[rows, qs]
            k = k_ref[rows, qs]
            v = v_ref[rows, vs]
            cc = colm[:, _P_C * hm + h:_P_C * hm + h + 1]
            winter = colm[:, _P_WINTER * hm + h:_P_WINTER * hm + h + 1]
            floor = colm[:, _P_FLOOR * hm + h:_P_FLOOR * hm + h + 1]
            wkf = colm[:, _P_WKF * hm + h:_P_WKF * hm + h + 1]
            decay = colm[0:1, _P_DECAY * hm + h:_P_DECAY * hm + h + 1]
            e_row = erow_ref[0, c, h:h + 1, :]

            qk = lax.dot_general(q, k, (((1,), (1,)), ((), ())), preferred_element_type=F32) * scale
            s = qk * jnp.exp(jnp.where(causal, e_row - cc, NEG))
            den_intra = jnp.sum(s, axis=1, keepdims=True)
            c_st = c_ref[h]
            n_st = n_ref[h]
            num = winter * jnp.dot(q, c_st.astype(BF16), preferred_element_type=F32) + jnp.dot(
                s.astype(BF16), v, preferred_element_type=F32)
            qn = jnp.sum(q.astype(F32) * n_st, axis=1, keepdims=True)
            den = winter * qn + den_intra
            hh = num * (1.0 / jnp.maximum(jnp.abs(den), floor))
            ms = jnp.mean(hh * hh, axis=1, keepdims=True)
            hn = hh * lax.rsqrt(ms + EPS) * g_ref[:, vs]
            og = o_ref[rows, vs].astype(F32)
            zg = z_ref[rows, vs].astype(F32)
            ya_ref[rows, vs] = (hn * _sigmoid(og) * (zg * _sigmoid(zg))).astype(ya_ref.dtype)

            wk = k.astype(F32) * (wkf * scale)
            upd = lax.dot_general(wk.astype(BF16), v, (((0,), (0,)), ((), ())), preferred_element_type=F32)
            c_ref[h] = decay * c_st + upd
            n_ref[h] = decay * n_st + jnp.sum(wk, axis=0, keepdims=True)
        return carry

    lax.fori_loop(0, t_chunks, chunk_body, 0)


def _mlstm(p, colm, erow, c0, n0, g, *, b, s, d, hm):
    dqk = d // 2 // hm
    dv = d // hm
    t_chunks = _pick(s // CHUNK, (4, 2, 1))
    rb = t_chunks * CHUNK
    nt = s // rb
    hw = d // 2
    row = lambda bi, t: bi * nt + t
    return pl.pallas_call(
        functools.partial(_mlstm_kernel, t_chunks=t_chunks, hm=hm, dqk=dqk, dv=dv, scale=float(dqk) ** -0.5),
        grid=(b, nt),
        in_specs=[pl.BlockSpec((rb, hw), lambda bi, t: (row(bi, t), 0)),
                  pl.BlockSpec((rb, hw), lambda bi, t: (row(bi, t), 1)),
                  pl.BlockSpec((rb, d), lambda bi, t: (row(bi, t), 1)),
                  pl.BlockSpec((rb, d), lambda bi, t: (row(bi, t), 2)),
                  pl.BlockSpec((rb, d), lambda bi, t: (row(bi, t), 3)),
                  pl.BlockSpec((1, rb, LANES), lambda bi, t: (bi, t, 0)),
                  pl.BlockSpec((1, t_chunks, hm, CHUNK), lambda bi, t: (bi, t, 0, 0)),
                  pl.BlockSpec((hm, dqk, dv), lambda bi, t: (0, 0, 0)),
                  pl.BlockSpec((hm, 1, dqk), lambda bi, t: (0, 0, 0)),
                  pl.BlockSpec((1, d), lambda bi, t: (0, 0))],
        out_specs=[pl.BlockSpec((rb, d), lambda bi, t: (row(bi, t), 0)),
                   pl.BlockSpec((None, hm, dqk, dv), lambda bi, t: (bi, 0, 0, 0)),
                   pl.BlockSpec((None, hm, 1, dqk), lambda bi, t: (bi, 0, 0, 0))],
        out_shape=[jax.ShapeDtypeStruct((b * s, d), BF16),
                   jax.ShapeDtypeStruct((b, hm, dqk, dv), F32),
                   jax.ShapeDtypeStruct((b, hm, 1, dqk), F32)],
        compiler_params=_params(("arbitrary", "arbitrary"), VMEM_LIMIT),
        name="mlstm",
    )(p, p, p, p, p, colm, erow, c0, n0, g.reshape(1, d).astype(F32))


def _gdn_kernel(q_ref, k_ref, v_ref, z_ref, colg_ref, grow_ref, s0_ref, g_ref, yb_ref, s_ref, qk_scr, u_scr, wq_scr,
                *, t_chunks, hg, dg):
    @pl.when(pl.program_id(2) == 0)
    def _():
        s_ref[...] = s0_ref[...]

    npk = hg // _PACK
    side = _PACK * CHUNK
    row = lax.broadcasted_iota(jnp.int32, (CHUNK, side), 0)
    lane = lax.broadcasted_iota(jnp.int32, (CHUNK, side), 1)
    lj = lane % CHUNK
    lgrp = lane // CHUNK
    incl = lj <= row
    strict = lj < row
    eye = (lj == row).astype(F32)
    bd_mask = (lax.broadcasted_iota(jnp.int32, (side, side), 0) // CHUNK
               == lax.broadcasted_iota(jnp.int32, (side, side), 1) // CHUNK)
    kgrp = lax.broadcasted_iota(jnp.int32, (CHUNK, _PACK * dg), 1) // dg
    tdims = (((0,), (0,)), ((), ()))

    def blockdiag(x):
        return jnp.where(bd_mask, jnp.concatenate([x] * _PACK, axis=0), jnp.zeros((), x.dtype))

    def spread(colg, j, g):
        base = j * hg + g * _PACK
        acc = jnp.broadcast_to(colg[:, base:base + 1], (CHUNK, side))
        for hq in range(1, _PACK):
            acc = jnp.where(lgrp == hq, colg[:, base + hq:base + hq + 1], acc)
        return acc

    units = [(c, g) for c in range(t_chunks) for g in range(npk)]
    us = range(len(units))
    rws = [slice(c * CHUNK, (c + 1) * CHUNK) for c, _ in units]
    lns = [slice(g * _PACK * dg, (g + 1) * _PACK * dg) for _, g in units]
    colgs = [colg_ref[0, rws[u], :] for u in us]
    k4 = [k_ref[rws[u], lns[u]] for u in us]
    q4 = [q_ref[rws[u], lns[u]] for u in us]
    kbd = [jnp.concatenate([jnp.where(kgrp == hq, k4[u], jnp.zeros((), k4[u].dtype)) for hq in range(_PACK)], axis=0)
           for u in us]
    qkkk = [lax.dot_general(jnp.concatenate([q4[u], k4[u]], axis=0), kbd[u], _NT, preferred_element_type=F32)
            for u in us]
    dec = [jnp.exp(jnp.where(incl, spread(colgs[u], 0, units[u][1])
                             - grow_ref[0, units[u][0], 0, 0, units[u][1]:units[u][1] + 1, :], NEG)) for u in us]
    for u in us:
        qk_scr[units[u][0], units[u][1]] = (qkkk[u][:CHUNK] * dec[u]).astype(qk_scr.dtype)
    a = [jnp.where(strict, qkkk[u][CHUNK:] * dec[u], 0.0) * spread(colgs[u], 1, units[u][1]) for u in us]
    tm = [eye - a[u] for u in us]
    ab = [a[u].astype(BF16) for u in us]
    pw = [jnp.dot(ab[u], blockdiag(ab[u]), preferred_element_type=F32) for u in us]
    for _ in range(4):
        pb = [pw[u].astype(BF16) for u in us]
        both = [jnp.dot(jnp.concatenate([tm[u].astype(BF16), pb[u]], axis=0), blockdiag(pb[u]),
                        preferred_element_type=F32) for u in us]
        tm = [tm[u] + both[u][:CHUNK] for u in us]
        pw = [both[u][CHUNK:] for u in us]
    tb = [((tm[u] + jnp.dot(tm[u].astype(BF16), blockdiag(pw[u].astype(BF16)), preferred_element_type=F32))
           * grow_ref[0, units[u][0], 0, 1, units[u][1]:units[u][1] + 1, :]).astype(BF16) for u in us]

    def pad_rows(x, hq):
        parts = []
        if hq:
            parts.append(jnp.zeros((hq * CHUNK, x.shape[1]), x.dtype))
        parts.append(x)
        if hq < _PACK - 1:
            parts.append(jnp.zeros(((_PACK - 1 - hq) * CHUNK, x.shape[1]), x.dtype))
        return jnp.concatenate(parts, axis=0)

    items = [(u, hq) for u in us for hq in range(_PACK)]
    heads = [units[u][1] * _PACK + hq for u, hq in items]
    hsl = [slice(h * dg, (h + 1) * dg) for h in heads]
    expgc = [colgs[u][:, 2 * hg + h:2 * hg + h + 1] for (u, _), h in zip(items, heads)]
    rhs = [jnp.concatenate([v_ref[rws[u], hsl[i]], (k_ref[rws[u], hsl[i]].astype(F32) * expgc[i]).astype(BF16)], axis=1)
           for i, (u, _) in enumerate(items)]
    uw = [jnp.dot(tb[u], pad_rows(rhs[i], hq), preferred_element_type=F32) for i, (u, hq) in enumerate(items)]
    for i, (u, _) in enumerate(items):
        c, h = units[u][0], heads[i]
        u_scr[c, h] = uw[i][:, :dg]
        wq_scr[c, h, 0:CHUNK, :] = uw[i][:, dg:].astype(wq_scr.dtype)
        wq_scr[c, h, CHUNK:2 * CHUNK, :] = (q_ref[rws[u], hsl[i]].astype(F32) * expgc[i]).astype(wq_scr.dtype)

    def chunk_body(c, carry):
        r0 = pl.multiple_of(c * CHUNK, CHUNK)
        rows = pl.ds(r0, CHUNK)
        colg = colg_ref[0, rows, :]
        hs = range(hg)
        sls = [slice(h * dg, (h + 1) * dg) for h in hs]
        col = lambda j, h: colg[:, j * hg + h:j * hg + h + 1]
        s_st = [s_ref[h] for h in hs]
        s_b = [s_st[h].astype(BF16) for h in hs]
        wqs = [jnp.dot(wq_scr[c, h], s_b[h], preferred_element_type=F32) for h in hs]
        v_nb = [(u_scr[c, h] - wqs[h][:CHUNK]).astype(BF16) for h in hs]
        o = [wqs[h][CHUNK:] + jnp.dot(qk_scr[c, h // _PACK], pad_rows(v_nb[h], h % _PACK), preferred_element_type=F32)
             for h in hs]
        k_dec = [(k_ref[rows, sls[h]].astype(F32) * col(3, h)).astype(BF16) for h in hs]
        for h in hs:
            expgl = colg[0:1, 4 * hg + h:4 * hg + h + 1]
            s_ref[h] = expgl * s_st[h] + lax.dot_general(k_dec[h], v_nb[h], tdims, preferred_element_type=F32)
        for h in hs:
            ms = jnp.mean(o[h] * o[h], axis=1, keepdims=True)
            zg = z_ref[rows, sls[h]].astype(F32)
            yb_ref[rows, sls[h]] = (o[h] * lax.rsqrt(ms + EPS) * g_ref[...] * (zg * _sigmoid(zg))).astype(yb_ref.dtype)
        return carry

    lax.fori_loop(0, t_chunks, chunk_body, 0, unroll=2 if t_chunks % 2 == 0 else 1)


def _gdn(p, colg, grow, s0, g, *, b, s, d, hgd, hg, qcol0):
    dg = d // hgd
    groups = hgd // hg
    gw = hg * dg
    npk = hg // _PACK
    side = _PACK * CHUNK
    t_chunks = _pick(s // CHUNK, (8, 4, 2, 1))
    rb = t_chunks * CHUNK
    nt = s // rb
    qb = qcol0 // gw
    row = lambda bi, t: bi * nt + t
    return pl.pallas_call(
        functools.partial(_gdn_kernel, t_chunks=t_chunks, hg=hg, dg=dg),
        grid=(b, groups, nt),
        in_specs=[pl.BlockSpec((rb, gw), lambda bi, gi, t: (row(bi, t), qb + gi)),
                  pl.BlockSpec((rb, gw), lambda bi, gi, t: (row(bi, t), qb + groups + gi)),
                  pl.BlockSpec((rb, gw), lambda bi, gi, t: (row(bi, t), qb + 2 * groups + gi)),
                  pl.BlockSpec((rb, gw), lambda bi, gi, t: (row(bi, t), qb + 3 * groups + gi)),
                  pl.BlockSpec((1, rb, LANES), lambda bi, gi, t: (bi, t, gi)),
                  pl.BlockSpec((1, t_chunks, 1, 2, npk, side), lambda bi, gi, t: (bi, t, gi, 0, 0, 0)),
                  pl.BlockSpec((hg, dg, dg), lambda bi, gi, t: (gi, 0, 0)),
                  pl.BlockSpec((1, dg), lambda bi, gi, t: (0, 0))],
        out_specs=[pl.BlockSpec((rb, gw), lambda bi, gi, t: (row(bi, t), gi)),
                   pl.BlockSpec((None, hg, dg, dg), lambda bi, gi, t: (bi, gi, 0, 0))],
        out_shape=[jax.ShapeDtypeStruct((b * s, d), BF16),
                   jax.ShapeDtypeStruct((b, hgd, dg, dg), F32)],
        scratch_shapes=[pltpu.VMEM((t_chunks, npk, CHUNK, side), BF16),
                        pltpu.VMEM((t_chunks, hg, CHUNK, dg), F32),
                        pltpu.VMEM((t_chunks, hg, 2 * CHUNK, dg), BF16)],
        compiler_params=_params(("arbitrary", "arbitrary", "arbitrary"), VMEM_LIMIT),
        name="gdn",
    )(p, p, p, p, colg, grow, s0, g.reshape(1, dg).astype(F32))


def _merge_kernel(ya_ref, yb_ref, wa_ref, wb_ref, ga_ref, gb_ref, o_ref):
    pa = jnp.dot(ya_ref[...], wa_ref[...], preferred_element_type=F32)
    pb = jnp.dot(yb_ref[...], wb_ref[...], preferred_element_type=F32)
    o_ref[...] = (_sigmoid(ga_ref[...].astype(F32)) * pa + _sigmoid(gb_ref[...].astype(F32)) * pb).astype(o_ref.dtype)


def _merge(ya, yb, wa, wb, p, *, gcol0):
    m, d = ya.shape
    tm = _pick(m, (256, 128, 64))
    tn = _pick(d, (1024, 512, 256, 128))
    ga0 = gcol0 // tn
    gb0 = (gcol0 + d) // tn
    return pl.pallas_call(
        _merge_kernel,
        grid=(d // tn, m // tm),
        in_specs=[pl.BlockSpec((tm, d), lambda j, i: (i, 0)),
                  pl.BlockSpec((tm, d), lambda j, i: (i, 0)),
                  pl.BlockSpec((d, tn), lambda j, i: (0, j)),
                  pl.BlockSpec((d, tn), lambda j, i: (0, j)),
                  pl.BlockSpec((tm, tn), lambda j, i: (i, ga0 + j)),
                  pl.BlockSpec((tm, tn), lambda j, i: (i, gb0 + j))],
        out_specs=pl.BlockSpec((tm, tn), lambda j, i: (i, j)),
        out_shape=jax.ShapeDtypeStruct((m, d), BF16),
        compiler_params=_params(("parallel", "parallel"), VMEM_LIMIT),
        name="merge",
    )(ya, yb, wa, wb, p, p)


def _outproj_kernel(m_ref, w_ref, x_ref, g_ref, o_ref, ss_ref, *, tn, nj, d):
    j = pl.program_id(1)
    y = x_ref[...] + jnp.dot(m_ref[...], w_ref[...], preferred_element_type=F32)
    part = jnp.sum(y * y, axis=1, keepdims=True)

    @pl.when(j == 0)
    def _():
        ss_ref[...] = part

    @pl.when(j > 0)
    def _():
        ss_ref[...] += part

    for jj in range(nj):
        @pl.when(j == jj)
        def _(jj=jj):
            o_ref[:, jj * tn:(jj + 1) * tn] = y

    @pl.when(j == nj - 1)
    def _():
        o_ref[...] = o_ref[...] * lax.rsqrt(ss_ref[...] / d + EPS) * g_ref[...]


def _outproj(merged, wo, x2d, g):
    m, d = merged.shape
    tm = _pick(m, (512, 256, 128, 64))
    tn = _pick(d, (512, 256, 128))
    nj = d // tn
    return pl.pallas_call(
        functools.partial(_outproj_kernel, tn=tn, nj=nj, d=d),
        grid=(m // tm, nj),
        in_specs=[pl.BlockSpec((tm, d), lambda i, j: (i, 0)),
                  pl.BlockSpec((d, tn), lambda i, j: (0, j)),
                  pl.BlockSpec((tm, tn), lambda i, j: (i, j)),
                  pl.BlockSpec((1, d), lambda i, j: (0, 0))],
        out_specs=pl.BlockSpec((tm, d), lambda i, j: (i, 0)),
        out_shape=jax.ShapeDtypeStruct((m, d), F32),
        scratch_shapes=[pltpu.VMEM((tm, 1), F32)],
        compiler_params=_params(("parallel", "arbitrary"), VMEM_LIMIT),
        name="outproj",
    )(merged, wo, x2d, g.reshape(1, d).astype(F32))


def _pad_lanes(a, width=LANES):
    return jnp.pad(a, [(0, 0)] * (a.ndim - 1) + [(0, width - a.shape[-1])])


def kernel(x, meta, norm_in_g, w_in, b_igate, b_fgate, ml_norm_g, conv_w, a_log, dt_bias,
           gdn_norm_g, w_proj_a, w_proj_b, w_out, norm_f_g):
    assert norm_in_g.shape[0] == 1, "single-layer block"
    b, s, d = x.shape
    hm = b_igate.shape[-1]
    hgd = a_log.shape[-1]
    dg = d // hgd
    hg = min(8, hgd)
    groups = hgd // hg
    assert s % CHUNK == 0 and meta.shape[0] == N_META and 2 * hm + 2 * hgd <= LANES
    n_chunks = s // CHUNK
    s_full = s + CHUNK

    wt = jnp.swapaxes(w_in[0], 0, 1)
    g0 = 4 * d
    g1 = g0 + 2 * hm
    g2 = g1 + 4 * d
    g3 = g2 + 2 * hgd
    w_gate = jnp.pad(jnp.concatenate([wt[g0:g1], wt[g2:g3]], axis=0), ((0, LANES - 2 * hm - 2 * hgd), (0, 0)))
    wa = w_proj_a[0].astype(BF16)
    wb = w_proj_b[0].astype(BF16)
    wo = w_out[0].astype(BF16)

    x2d = x.reshape(b * s, d)
    h0 = jnp.concatenate([jnp.zeros((N_PAD, d), x.dtype), meta.astype(x.dtype)], axis=0)
    hn_main, gs_main = _norm_gate(x2d, norm_in_g[0], w_gate)
    hn_meta, gs_meta = _norm_gate(h0, norm_in_g[0], w_gate)
    p_main, p_meta = _inproj(hn_main, hn_meta, wt, conv_w[0], n_out=10 * d, seg_starts=(4 * d, 8 * d),
                             seg_skip=(2 * hm, 2 * hgd), conv_col0=4 * d, seq_rows=s, dg=dg)

    gs_full = jnp.concatenate([jnp.broadcast_to(gs_meta[None], (b, CHUNK, LANES)),
                               gs_main.reshape(b, s, LANES)], axis=1)
    lane_g = (2 * hm, LANES - 2 * hm - hgd)
    prm = jnp.pad(jnp.stack([_pad_lanes(b_igate[0].astype(F32)), _pad_lanes(b_fgate[0].astype(F32)),
                             jnp.pad(a_log[0].astype(F32), lane_g), jnp.pad(dt_bias[0].astype(F32), lane_g)]),
                  ((0, SUBLANES - 4), (0, 0)))
    colm, colg, small = _gates(gs_full, prm, hm=hm, hgd=hgd, hg=hg)
    erow = small[:, 0, :, 0:hm].reshape(b, n_chunks + 1, CHUNK, hm).transpose(0, 1, 3, 2)
    grow = small[:, 1:3, :, 2 * hm:2 * hm + hgd].reshape(b, 2, n_chunks + 1, CHUNK, groups, hg)
    grow = grow.transpose(0, 2, 4, 1, 5, 3).reshape(b, n_chunks + 1, groups, 2, hg // _PACK, _PACK * CHUNK)

    dqk = d // 2 // hm
    dv = d // hm
    c0 = jnp.zeros((hm, dqk, dv), F32)
    n0 = jnp.zeros((hm, 1, dqk), F32)
    _, c1, n1 = _mlstm(p_meta, colm[:1, :CHUNK], erow[:1, :1], c0, n0, ml_norm_g[0], b=1, s=CHUNK, d=d, hm=hm)
    ya, _, _ = _mlstm(p_main, colm[:, CHUNK:], erow[:, 1:], c1[0], n1[0], ml_norm_g[0], b=b, s=s, d=d, hm=hm)

    s0 = jnp.zeros((hgd, dg, dg), F32)
    _, s1 = _gdn(p_meta, colg[:1, :CHUNK], grow[:1, :1], s0, gdn_norm_g[0],
                 b=1, s=CHUNK, d=d, hgd=hgd, hg=hg, qcol0=4 * d)
    yb, _ = _gdn(p_main, colg[:, CHUNK:], grow[:, 1:], s1[0], gdn_norm_g[0],
                 b=b, s=s, d=d, hgd=hgd, hg=hg, qcol0=4 * d)

    merged = _merge(ya, yb, wa, wb, p_main, gcol0=8 * d)
    return _outproj(merged, wo, x2d, norm_f_g).reshape(b, s, d)
```
